```python
import jax, jax.numpy as jnp
from jax import lax
import numpy as np

D_MODEL = 1024
BATCH = 4
SEQ = 4096
DEPTH = 2
DEC_BATCH = 128
DEC_SEQ = 1
PAST_LEN = 16384
PAGE_SIZE = 128

N_META = 16
HEAD_DIM = 64
SB_HEADS = 8
SB_KV_HEADS = 2
MLA_HEADS = 8
MLA_Q_LORA = 256
MLA_KV_LORA = 192
MLA_NOPE = 64
MLA_ROPE = 32
MLA_V = 64
DSA_HEADS = 8
IDX_HEADS = 16
IDX_DIM = 64
TOPK_MAX = 256
D_FF = 4 * D_MODEL
ROPE_THETA = 10000.0
Q_BLOCK = 128
EPS = 1e-6
SB_SCALE = HEAD_DIM ** -0.5
MLA_SCALE = (MLA_NOPE + MLA_ROPE) ** -0.5
DSA_SCALE = HEAD_DIM ** -0.5
IN_SIZES = (SB_HEADS * HEAD_DIM, SB_KV_HEADS * HEAD_DIM, SB_KV_HEADS * HEAD_DIM,
            MLA_Q_LORA, MLA_KV_LORA, MLA_ROPE,
            DSA_HEADS * HEAD_DIM, HEAD_DIM, HEAD_DIM, IDX_HEADS * IDX_DIM, IDX_DIM, IDX_HEADS,
            3 * D_MODEL)
IN_WIDTH = sum(IN_SIZES)

kernel_name = 'hybrid_sb_mla_dsa_decoder_step'


def rms_norm(x, g):
    xf = x.astype(jnp.float32)
    y = xf * lax.rsqrt(jnp.mean(xf * xf, axis=-1, keepdims=True) + EPS)
    return (y * g.astype(jnp.float32)).astype(x.dtype)


def rope(x, pos):
    half = x.shape[-1] // 2
    inv_freq = ROPE_THETA ** (-jnp.arange(half, dtype=jnp.float32) / half)
    ang = pos.astype(jnp.float32)[:, None] * inv_freq[None, :]
    cos, sin = jnp.cos(ang)[:, None, :], jnp.sin(ang)[:, None, :]
    xf = x.astype(jnp.float32)
    x1, x2 = xf[..., :half], xf[..., half:]
    return jnp.concatenate([x1 * cos - x2 * sin, x1 * sin + x2 * cos], axis=-1).astype(x.dtype)


def masked_softmax(z, mask):
    return jax.nn.softmax(jnp.where(mask, z, -jnp.inf), axis=-1)


def stick_breaking_weights(z, mask):
    log_beta = jax.nn.log_sigmoid(z)
    log_fail = jnp.where(mask, jax.nn.log_sigmoid(-z), 0.0)
    tail = lax.cumsum(log_fail, axis=z.ndim - 1, reverse=True) - log_fail
    return jnp.where(mask, jnp.exp(log_beta + tail), 0.0)


def to_blocks(a):
    b, lp = a.shape[:2]
    return jnp.moveaxis(a.reshape(b, lp // Q_BLOCK, Q_BLOCK, *a.shape[2:]), 1, 0)


def from_blocks(a):
    a = jnp.moveaxis(a, 0, 1)
    return a.reshape(a.shape[0], a.shape[1] * a.shape[2], *a.shape[3:])


def project(h, pos, w_in, q_norm, kv_norm, w_uq, w_uk):
    b, l, _ = h.shape
    splits = [int(s) for s in np.cumsum(IN_SIZES)[:-1]]
    (sb_q, sb_k, sb_v, mla_cq, mla_ckv, mla_kr, dsa_q, dsa_k, dsa_v,
     idx_q, idx_k, idx_w, gates) = jnp.split(h @ w_in, splits, axis=-1)
    sb_q = sb_q.reshape(b, l, SB_HEADS, HEAD_DIM)
    sb_k = sb_k.reshape(b, l, SB_KV_HEADS, HEAD_DIM)
    sb_v = sb_v.reshape(b, l, SB_KV_HEADS, HEAD_DIM)
    q = jnp.einsum('blc,chd->blhd', rms_norm(mla_cq, q_norm), w_uq)
    q_lat = jnp.einsum('blhn,chn->blhc', q[..., :MLA_NOPE], w_uk)
    mla_q = jnp.concatenate([q_lat, rope(q[..., MLA_NOPE:], pos)], axis=-1) * MLA_SCALE
    k_rot = rope(mla_kr[:, :, None, :], pos)[:, :, 0]
    mla_row = jnp.concatenate([rms_norm(mla_ckv, kv_norm), k_rot], axis=-1)
    dsa_q = rope(dsa_q.reshape(b, l, DSA_HEADS, HEAD_DIM), pos)
    dsa_k = rope(dsa_k[:, :, None, :], pos)[:, :, 0]
    idx_q = rope(idx_q.reshape(b, l, IDX_HEADS, IDX_DIM), pos) * (IDX_DIM ** -0.5)
    idx_k = rope(idx_k[:, :, None, :], pos)[:, :, 0]
    idx_w = idx_w * (IDX_HEADS ** -0.5)
    gates = jax.nn.sigmoid(gates.reshape(b, l, 3, D_MODEL))
    return sb_q, sb_k, sb_v, mla_q, mla_row, dsa_q, dsa_k, dsa_v, idx_q, idx_k, idx_w, gates


def merge(y_sb, y_mla, y_dsa, gates, w_bsb, w_bmla, w_bdsa, w_out):
    m = (gates[:, :, 0] * (y_sb @ w_bsb) + gates[:, :, 1] * (y_mla @ w_bmla)
         + gates[:, :, 2] * (y_dsa @ w_bdsa))
    return m @ w_out


def sq_relu_mlp(h, w_up, w_down):
    return jnp.square(jax.nn.relu(h @ w_up)) @ w_down


def mla_up(o_lat, w_uv):
    b, l = o_lat.shape[:2]
    return jnp.einsum('blhc,chv->blhv', o_lat, w_uv).reshape(b, l, MLA_HEADS * MLA_V)


def sb_prompt(q, k, v):
    b, lp = q.shape[:2]
    g = SB_HEADS // SB_KV_HEADS
    qg = q.reshape(b, lp, SB_KV_HEADS, g, HEAD_DIM) * SB_SCALE
    kpos = jnp.arange(lp)

    def block(args):
        qb, start = args
        qpos = start + jnp.arange(Q_BLOCK)
        z = jnp.einsum('bqhgd,bkhd->bhgqk', qb, k, preferred_element_type=jnp.float32)
        a = stick_breaking_weights(z, kpos[None, :] < qpos[:, None])
        return jnp.einsum('bhgqk,bkhd->bqhgd', a.astype(v.dtype), v)

    starts = jnp.arange(lp // Q_BLOCK) * Q_BLOCK
    o = lax.map(block, (to_blocks(qg), starts))
    return from_blocks(o).reshape(b, lp, SB_HEADS * HEAD_DIM)


def mla_prompt(q, rows):
    lp = q.shape[1]
    kpos = jnp.arange(lp)

    def block(args):
        qb, start = args
        qpos = start + jnp.arange(Q_BLOCK)
        z = jnp.einsum('bqhc,bkc->bhqk', qb, rows, preferred_element_type=jnp.float32)
        p = masked_softmax(z, kpos[None, :] <= qpos[:, None]).astype(rows.dtype)
        return jnp.einsum('bhqk,bkc->bqhc', p, rows)[..., :MLA_KV_LORA]

    starts = jnp.arange(lp // Q_BLOCK) * Q_BLOCK
    return from_blocks(lax.map(block, (to_blocks(q), starts)))


def select_keys(score, mask, n_sel, qpos):
    _, idx = lax.top_k(jnp.where(mask, score, -jnp.inf), n_sel)
    return idx, idx <= qpos[:, None]


def dsa_attend(q, k_sel, v_sel, valid):
    b, nq = q.shape[:2]
    z = jnp.einsum('bqhd,bqnd->bqhn', q, k_sel, preferred_element_type=jnp.float32) * DSA_SCALE
    p = masked_softmax(z, valid[:, :, None, :]).astype(v_sel.dtype)
    return jnp.einsum('bqhn,bqnd->bqhd', p, v_sel).reshape(b, nq, DSA_HEADS * HEAD_DIM)


def dsa_prompt(q, k, v, iq, ik, iw, n_sel):
    lp = q.shape[1]
    kpos = jnp.arange(lp)
    gather = jax.vmap(lambda rows, idx: rows[idx])

    def block(args):
        qb, iqb, iwb, start = args
        qpos = start + jnp.arange(Q_BLOCK)
        s = jnp.einsum('bqhe,bke->bqhk', iqb, ik, preferred_element_type=jnp.float32)
        score = jnp.einsum('bqhk,bqh->bqk', jax.nn.relu(s), iwb)
        idx, valid = select_keys(score, kpos[None, :] <= qpos[:, None], n_sel, qpos)
        return dsa_attend(qb, gather(k, idx), gather(v, idx), valid)

    starts = jnp.arange(lp // Q_BLOCK) * Q_BLOCK
    o = lax.map(block, (to_blocks(q), to_blocks(iq), to_blocks(iw.astype(jnp.float32)), starts))
    return from_blocks(o)


def mixer_prompt(h, pos, n_sel, w_in, q_norm, kv_norm, w_uq, w_uk, w_uv,
                 w_bsb, w_bmla, w_bdsa, w_out):
    (sb_q, sb_k, sb_v, mla_q, mla_row, dsa_q, dsa_k, dsa_v,
     idx_q, idx_k, idx_w, gates) = project(h, pos, w_in, q_norm, kv_norm, w_uq, w_uk)
    y_sb = sb_prompt(sb_q, sb_k, sb_v)
    y_mla = mla_up(mla_prompt(mla_q, mla_row), w_uv)
    y_dsa = dsa_prompt(dsa_q, dsa_k, dsa_v, idx_q, idx_k, idx_w, n_sel)
    out = merge(y_sb, y_mla, y_dsa, gates, w_bsb, w_bmla, w_bdsa, w_out)
    return out, (sb_k, sb_v, mla_row, dsa_k, dsa_v, idx_k)


def sb_sample(q, k_new, v_new, k_past, v_past):
    db, ds = q.shape[:2]
    p_len = k_past.shape[1]
    g = SB_HEADS // SB_KV_HEADS
    qg = q.reshape(db, ds, SB_KV_HEADS, g, HEAD_DIM) * SB_SCALE
    z = jnp.concatenate([
        jnp.einsum('bqhgd,bkhd->bhgqk', qg, k_past, preferred_element_type=jnp.float32),
        jnp.einsum('bqhgd,bkhd->bhgqk', qg, k_new, preferred_element_type=jnp.float32)], axis=-1)
    kpos = jnp.arange(p_len + ds)
    qpos = p_len + jnp.arange(ds)
    a = stick_breaking_weights(z, kpos[None, :] < qpos[:, None]).astype(v_new.dtype)
    o = (jnp.einsum('bhgqk,bkhd->bqhgd', a[..., :p_len], v_past)
         + jnp.einsum('bhgqk,bkhd->bqhgd', a[..., p_len:], v_new))
    return o.reshape(db, ds, SB_HEADS * HEAD_DIM)


def mla_sample(q, rows_new, rows_past):
    ds = q.shape[1]
    p_len = rows_past.shape[1]
    z = jnp.concatenate([
        jnp.einsum('bqhc,bkc->bhqk', q, rows_past, preferred_element_type=jnp.float32),
        jnp.einsum('bqhc,bkc->bhqk', q, rows_new, preferred_element_type=jnp.float32)], axis=-1)
    kpos = jnp.arange(p_len + ds)
    qpos = p_len + jnp.arange(ds)
    p = masked_softmax(z, kpos[None, :] <= qpos[:, None]).astype(rows_new.dtype)
    o = (jnp.einsum('bhqk,bkc->bqhc', p[..., :p_len], rows_past)
         + jnp.einsum('bhqk,bkc->bqhc', p[..., p_len:], rows_new))
    return o[..., :MLA_KV_LORA]


def dsa_sample(q, k_new, v_new, iq, ik_new, iw, ik_past, cache_k, cache_v, layer, page_table, n_sel):
    db, ds = q.shape[:2]
    p_len = ik_past.shape[1]
    iwf = iw.astype(jnp.float32)

    def idx_score(ik):
        s = jnp.einsum('bqhe,bke->bqhk', iq, ik, preferred_element_type=jnp.float32)
        return jnp.einsum('bqhk,bqh->bqk', jax.nn.relu(s), iwf)

    score = jnp.concatenate([idx_score(ik_past), idx_score(ik_new)], axis=-1)
    kpos = jnp.arange(p_len + ds)
    qpos = p_len + jnp.arange(ds)
    idx, valid = select_keys(score, kpos[None, :] <= qpos[:, None], n_sel, qpos)
    bi = jnp.arange(db)[:, None, None]
    past_i = jnp.minimum(idx, p_len - 1)
    new_i = jnp.clip(idx - p_len, 0, ds - 1)
    phys = page_table[bi, past_i // PAGE_SIZE]
    off = past_i % PAGE_SIZE
    in_past = (idx < p_len)[..., None]
    k_sel = jnp.where(in_past, cache_k[layer, phys, off], k_new[bi, new_i])
    v_sel = jnp.where(in_past, cache_v[layer, phys, off], v_new[bi, new_i])
    return dsa_attend(q, k_sel, v_sel, valid)


def mixer_sample(h, pos, n_sel, layer, cache_sb_k, cache_sb_v, cache_mla, cache_dsa_k, cache_dsa_v,
                 cache_dsa_idx, page_table, w_in, q_norm, kv_norm, w_uq, w_uk, w_uv,
                 w_bsb, w_bmla, w_bdsa, w_out):
    db = h.shape[0]
    p_len = page_table.shape[1] * PAGE_SIZE

    def past(cache):
        g = cache[layer, page_table]
        return g.reshape(db, p_len, *g.shape[3:])

    (sb_q, sb_k, sb_v, mla_q, mla_row, dsa_q, dsa_k, dsa_v,
     idx_q, idx_k, idx_w, gates) = project(h, pos, w_in, q_norm, kv_norm, w_uq, w_uk)
    y_sb = sb_sample(sb_q, sb_k, sb_v, past(cache_sb_k), past(cache_sb_v))
    y_mla = mla_up(mla_sample(mla_q, mla_row, past(cache_mla)), w_uv)
    y_dsa = dsa_sample(dsa_q, dsa_k, dsa_v, idx_q, idx_k, idx_w, past(cache_dsa_idx),
                       cache_dsa_k, cache_dsa_v, layer, page_table, n_sel)
    out = merge(y_sb, y_mla, y_dsa, gates, w_bsb, w_bmla, w_bdsa, w_out)
    return out, (sb_k, sb_v, mla_row, dsa_k, dsa_v, idx_k)


def setup_inputs(seed: int = 0) -> dict:
    key = jax.random.key(seed)
    ks = jax.random.split(key, 32)
    n_pages = PAST_LEN // PAGE_SIZE
    n_used = DEC_BATCH * n_pages
    n_pool = n_used + n_used // 4
    f32 = jnp.float32

    def nrm(k, shape, scale=1.0):
        return jax.random.normal(k, shape, f32) * scale

    def gain(k, shape):
        return 1.0 + 0.01 * jax.random.normal(k, shape, f32)

    page_table = jax.random.permutation(ks[8], n_pool)[:n_used].reshape(DEC_BATCH, n_pages).astype(jnp.int32)
    return {
        'x_prompt': nrm(ks[0], (BATCH, SEQ, D_MODEL)),
        'x_sample': nrm(ks[1], (DEC_BATCH, DEC_SEQ, D_MODEL)),
        'cache_sb_k': nrm(ks[2], (DEPTH, n_pool, PAGE_SIZE, SB_KV_HEADS, HEAD_DIM)),
        'cache_sb_v': nrm(ks[3], (DEPTH, n_pool, PAGE_SIZE, SB_KV_HEADS, HEAD_DIM)),
        'cache_mla': nrm(ks[4], (DEPTH, n_pool, PAGE_SIZE, MLA_KV_LORA + MLA_ROPE)),
        'cache_dsa_k': nrm(ks[5], (DEPTH, n_pool, PAGE_SIZE, HEAD_DIM)),
        'cache_dsa_v': nrm(ks[6], (DEPTH, n_pool, PAGE_SIZE, HEAD_DIM)),
        'cache_dsa_idx': nrm(ks[7], (DEPTH, n_pool, PAGE_SIZE, IDX_DIM)),
        'page_table': page_table,
        'meta': nrm(ks[9], (N_META, D_MODEL)),
        'norm_mix': gain(ks[10], (DEPTH, D_MODEL)),
        'w_in': nrm(ks[11], (DEPTH, D_MODEL, IN_WIDTH), D_MODEL ** -0.5),
        'mla_q_norm': gain(ks[12], (DEPTH, MLA_Q_LORA)),
        'mla_kv_norm': gain(ks[13], (DEPTH, MLA_KV_LORA)),
        'mla_w_uq': nrm(ks[14], (DEPTH, MLA_Q_LORA, MLA_HEADS, MLA_NOPE + MLA_ROPE), MLA_Q_LORA ** -0.5),
        'mla_w_uk': nrm(ks[15], (DEPTH, MLA_KV_LORA, MLA_HEADS, MLA_NOPE), MLA_KV_LORA ** -0.5),
        'mla_w_uv': nrm(ks[16], (DEPTH, MLA_KV_LORA, MLA_HEADS, MLA_V), MLA_KV_LORA ** -0.5),
        'w_branch_sb': nrm(ks[17], (DEPTH, SB_HEADS * HEAD_DIM, D_MODEL), (SB_HEADS * HEAD_DIM) ** -0.5),
        'w_branch_mla': nrm(ks[18], (DEPTH, MLA_HEADS * MLA_V, D_MODEL), (MLA_HEADS * MLA_V) ** -0.5),
        'w_branch_dsa': nrm(ks[19], (DEPTH, DSA_HEADS * HEAD_DIM, D_MODEL), (DSA_HEADS * HEAD_DIM) ** -0.5),
        'w_out': nrm(ks[20], (DEPTH, D_MODEL, D_MODEL), D_MODEL ** -0.5),
        'norm_mlp': gain(ks[21], (DEPTH, D_MODEL)),
        'w_up': nrm(ks[22], (DEPTH, D_MODEL, D_FF), D_MODEL ** -0.5),
        'w_down': nrm(ks[23], (DEPTH, D_FF, D_MODEL), D_FF ** -0.5),
        'norm_final': gain(ks[24], (D_MODEL,)),
    }


def reference(x_prompt, x_sample, cache_sb_k, cache_sb_v, cache_mla, cache_dsa_k, cache_dsa_v,
              cache_dsa_idx, page_table, meta, norm_mix, w_in, mla_q_norm, mla_kv_norm,
              mla_w_uq, mla_w_uk, mla_w_uv, w_branch_sb, w_branch_mla, w_branch_dsa, w_out,
              norm_mlp, w_up, w_down, norm_final):
    b, seq = x_prompt.shape[:2]
    ds = x_sample.shape[1]
    p_len = page_table.shape[1] * PAGE_SIZE
    t_len = seq + N_META
    lp = -(-t_len // Q_BLOCK) * Q_BLOCK
    xp = jnp.concatenate([jnp.broadcast_to(meta.astype(x_prompt.dtype)[None], (b, N_META, D_MODEL)),
                          x_prompt], axis=1)
    xp = jnp.pad(xp, ((0, 0), (0, lp - t_len), (0, 0)))
    xs = x_sample
    pos_p = jnp.arange(lp)
    pos_s = p_len + jnp.arange(ds)
    n_sel_p = min(TOPK_MAX, seq // 4)
    n_sel_s = min(TOPK_MAX, (p_len + ds) // 4)

    prompt_rows = [[] for _ in range(6)]
    sample_rows = [[] for _ in range(6)]
    for l in range(DEPTH):
        shared = (w_in[l], mla_q_norm[l], mla_kv_norm[l], mla_w_uq[l], mla_w_uk[l], mla_w_uv[l],
                  w_branch_sb[l], w_branch_mla[l], w_branch_dsa[l], w_out[l])
        dp, rows_p = mixer_prompt(rms_norm(xp, norm_mix[l]), pos_p, n_sel_p, *shared)
        xp = xp + dp
        xp = xp + sq_relu_mlp(rms_norm(xp, norm_mlp[l]), w_up[l], w_down[l])
        dsm, rows_s = mixer_sample(rms_norm(xs, norm_mix[l]), pos_s, n_sel_s, l, cache_sb_k, cache_sb_v,
                                   cache_mla, cache_dsa_k, cache_dsa_v, cache_dsa_idx, page_table, *shared)
        xs = xs + dsm
        xs = xs + sq_relu_mlp(rms_norm(xs, norm_mlp[l]), w_up[l], w_down[l])
        for i in range(6):
            prompt_rows[i].append(rows_p[i][:, :t_len])
            sample_rows[i].append(rows_s[i])

    y_prompt = rms_norm(xp[:, N_META:t_len], norm_final)
    y_sample = rms_norm(xs, norm_final)
    p_sb_k, p_sb_v, p_mla, p_dsa_k, p_dsa_v, p_dsa_idx = [jnp.stack(r, axis=0) for r in prompt_rows]
    s_sb_k, s_sb_v, s_mla, s_dsa_k, s_dsa_v, s_dsa_idx = [jnp.stack(r, axis=0) for r in sample_rows]
    return (y_prompt, y_sample, p_sb_k, p_sb_v, p_mla, p_dsa_k, p_dsa_v, p_dsa_idx,
            s_sb_k, s_sb_v, s_mla, s_dsa_k, s_dsa_v, s_dsa_idx)
```

```python
import functools

import jax
import jax.numpy as jnp
from jax import lax
from jax.experimental import pallas as pl
from jax.experimental.pallas import tpu as pltpu

D_MODEL = 1024
N_META = 16
HEAD_DIM = 64
SB_HEADS = 8
SB_KV_HEADS = 2
MLA_HEADS = 8
MLA_Q_LORA = 256
MLA_KV_LORA = 192
MLA_NOPE = 64
MLA_ROPE = 32
MLA_V = 64
DSA_HEADS = 8
IDX_HEADS = 16
IDX_DIM = 64
TOPK_MAX = 256
D_FF = 4 * D_MODEL
ROPE_THETA = 10000.0
EPS = 1e-6
PAGE_SIZE = 128
SB_SCALE = HEAD_DIM ** -0.5
MLA_SCALE = (MLA_NOPE + MLA_ROPE) ** -0.5
DSA_SCALE = HEAD_DIM ** -0.5
IDX_SCALE = IDX_DIM ** -0.5
IDX_W_SCALE = IDX_HEADS ** -0.5

LANE = 128
VMEM_LIMIT = 56 * 1024 * 1024

MLA_ROW = 256
ROW_TILE = 256
ATT_TQ = 256
SB_TQ = 128
SB_DEAD = -104.0
NEG_BIG = -1e30
INT_MIN = -(2 ** 31)

f32 = jnp.float32
bf16 = jnp.bfloat16
i32 = jnp.int32

_NT = (((1,), (1,)), ((), ()))


def _dot(a, b):
    return jnp.dot(a, b, preferred_element_type=f32)


def _dot_nt(a, b):
    return lax.dot_general(a, b, _NT, preferred_element_type=f32)


def _params(n_grid):
    return pltpu.CompilerParams(dimension_semantics=("arbitrary",) * n_grid,
                                vmem_limit_bytes=VMEM_LIMIT)


def _full(shape):
    nd = len(shape)
    return pl.BlockSpec(shape, lambda *_: (0,) * nd)


def _norm_rows(x, g):
    ms = jnp.mean(x * x, axis=-1, keepdims=True)
    return x * lax.rsqrt(ms + EPS) * g


def _lane_iota(shape):
    return lax.broadcasted_iota(i32, shape, len(shape) - 1)


def _rope_tile(x, cos, sin, dim):
    half = dim // 2
    first = (_lane_iota(x.shape) % dim) < half
    swapped = jnp.where(first, pltpu.roll(x, LANE - half, 1), pltpu.roll(x, half, 1))
    return x * cos + swapped * sin


def _log1p_exp_neg_abs(z):
    return jnp.log1p(jnp.exp(-jnp.abs(z)))


def _score_key(score):
    b = pltpu.bitcast(score, i32)
    return b ^ ((b >> 31) & 0x7FFFFFFF)


def _split_heads_to_lanes_lo(y, lo):
    zero = jnp.zeros_like(y)
    return jnp.where(lo, y, zero), jnp.where(lo, pltpu.roll(y, 64, 1), zero)


def _proj_sb_kernel(x_ref, g_ref, w_ref, q_ref, k_ref, v_ref):
    xn = _norm_rows(x_ref[...], g_ref[...]).astype(bf16)
    acc = _dot(xn, w_ref[...])
    tm = acc.shape[0]
    lo = _lane_iota((tm, LANE)) < 64
    for p in range(SB_HEADS // 2):
        y = acc[:, p * LANE:(p + 1) * LANE] * SB_SCALE
        yr = pltpu.roll(y, 64, 1)
        zero = jnp.zeros_like(y)
        if p < 2:
            even, odd = jnp.where(lo, y, zero), jnp.where(lo, yr, zero)
        else:
            even, odd = jnp.where(lo, zero, yr), jnp.where(lo, zero, y)
        q_ref[:, (2 * p) * LANE:(2 * p + 1) * LANE] = even.astype(bf16)
        q_ref[:, (2 * p + 1) * LANE:(2 * p + 2) * LANE] = odd.astype(bf16)
    k_ref[...] = acc[:, 512:640]
    v_ref[...] = acc[:, 640:768]


def _proj_mla_kernel(x_ref, g_ref, w_ref, qn_ref, kvn_ref, wuq_ref, wabs_ref, place_ref,
                     cos_ref, sin_ref, q_ref, row_ref, rowb_ref):
    xn = _norm_rows(x_ref[...], g_ref[...]).astype(bf16)
    acc = _dot(xn, w_ref[...])
    tm = acc.shape[0]
    cos, sin = cos_ref[...], sin_ref[...]
    cqn = _norm_rows(acc[:, :MLA_Q_LORA], qn_ref[...]).astype(bf16)
    qq = _dot(cqn, wuq_ref[...])
    q_lat = _dot(qq[:, :512].astype(bf16), wabs_ref[...])
    q_rot = jnp.concatenate(
        [_rope_tile(qq[:, 512 + t * LANE:512 + (t + 1) * LANE], cos, sin, MLA_ROPE) for t in range(2)],
        axis=1)
    placed = _dot((q_rot * MLA_SCALE).astype(bf16), place_ref[...])
    q_ref[...] = (q_lat * MLA_SCALE + placed).astype(bf16)
    g2 = acc[:, 256:512]
    lane2 = _lane_iota((tm, MLA_ROW))
    ms = jnp.sum(jnp.where(lane2 < MLA_KV_LORA, g2 * g2, 0.0), axis=-1, keepdims=True) / MLA_KV_LORA
    normed = g2 * lax.rsqrt(ms + EPS) * kvn_ref[...]
    roped = _rope_tile(g2[:, LANE:], cos, sin, MLA_ROPE)
    lane = _lane_iota((tm, LANE))
    tile2 = jnp.where(lane < 64, normed[:, LANE:], jnp.where(lane < 96, roped, 0.0))
    row = jnp.concatenate([normed[:, :LANE], tile2], axis=1)
    row_ref[...] = row
    rowb_ref[...] = row.astype(bf16)


def _proj_dsa_kernel(x_ref, g_ref, w_ref, cos_ref, sin_ref,
                     qd_ref, kv_ref, kvb_ref, qi_ref, ikw_ref, ikwb_ref):
    xn = _norm_rows(x_ref[...], g_ref[...]).astype(bf16)
    acc = _dot(xn, w_ref[...])
    tm = acc.shape[0]
    cos, sin = cos_ref[...], sin_ref[...]
    lane = _lane_iota((tm, LANE))
    lo = lane < 64
    for p in range(DSA_HEADS // 2):
        y = _rope_tile(acc[:, p * LANE:(p + 1) * LANE], cos, sin, HEAD_DIM) * DSA_SCALE
        even, odd = _split_heads_to_lanes_lo(y, lo)
        qd_ref[:, (2 * p) * LANE:(2 * p + 1) * LANE] = even.astype(bf16)
        qd_ref[:, (2 * p + 1) * LANE:(2 * p + 2) * LANE] = odd.astype(bf16)
    kv = acc[:, 512:640]
    kv = jnp.where(lo, _rope_tile(kv, cos, sin, HEAD_DIM), kv)
    kv_ref[...] = kv
    kvb_ref[...] = kv.astype(bf16)
    for p in range(IDX_HEADS // 2):
        y = _rope_tile(acc[:, 640 + p * LANE:640 + (p + 1) * LANE], cos, sin, IDX_DIM) * IDX_SCALE
        even, odd = _split_heads_to_lanes_lo(y, lo)
        qi_ref[:, (2 * p) * LANE:(2 * p + 1) * LANE] = even.astype(bf16)
        qi_ref[:, (2 * p + 1) * LANE:(2 * p + 2) * LANE] = odd.astype(bf16)
    ikw = acc[:, 1664:1792]
    ikw = jnp.where(lo, _rope_tile(ikw, cos, sin, IDX_DIM),
                    jnp.where(lane < 64 + IDX_HEADS, ikw * IDX_W_SCALE, 0.0))
    ikw_ref[...] = ikw
    ikwb_ref[...] = ikw.astype(bf16)


def _merge_kernel(x_ref, g_ref, wg_ref, ysb_ref, ymla_ref, ydsa_ref,
                  wsb_ref, wmla_ref, wdsa_ref, wout_ref, o_ref):
    x = x_ref[...]
    xn = _norm_rows(x, g_ref[...]).astype(bf16)
    gates = jax.nn.sigmoid(_dot(xn, wg_ref[...]))
    m = (gates[:, :D_MODEL] * _dot(ysb_ref[...], wsb_ref[...])
         + gates[:, D_MODEL:2 * D_MODEL] * _dot(ymla_ref[...], wmla_ref[...])
         + gates[:, 2 * D_MODEL:] * _dot(ydsa_ref[...], wdsa_ref[...]))
    o_ref[...] = x + _dot(m.astype(bf16), wout_ref[...])


def _mlp_kernel(x_ref, g_ref, wup_ref, wdown_ref, gf_ref, o_ref, *, final_norm):
    x = x_ref[...]
    xn = _norm_rows(x, g_ref[...]).astype(bf16)
    h = jnp.maximum(_dot(xn, wup_ref[...]), 0.0)
    y = x + _dot((h * h).astype(bf16), wdown_ref[...])
    if final_norm:
        y = _norm_rows(y, gf_ref[...])
    o_ref[...] = y


def _matmul_kernel(a_ref, b_ref, o_ref):
    o_ref[...] = _dot(a_ref[...], b_ref[...]).astype(o_ref.dtype)


def _sb_prompt_kernel(q_ref, k_ref, v_ref, tri_ref, o_ref, carry_sc, acc_sc, *, tq):
    qi = pl.program_id(1)
    heads = SB_HEADS
    rows = heads * tq
    q = jnp.concatenate([q_ref[:, h * LANE:(h + 1) * LANE] for h in range(heads)], axis=0)
    qpos = qi * tq + lax.broadcasted_iota(i32, (rows, LANE), 0) % tq
    lane = _lane_iota((rows, LANE))
    carry_sc[...] = jnp.zeros_like(carry_sc)
    acc_sc[...] = jnp.zeros_like(acc_sc)

    def cond(s):
        j, alive = s
        return jnp.logical_and(j >= 0, alive > 0)

    def body(s):
        j, _ = s
        off = pl.multiple_of(j * LANE, LANE)
        k = k_ref[pl.ds(off, LANE), :].astype(bf16)
        v = v_ref[pl.ds(off, LANE), :].astype(bf16)
        z = _dot_nt(q, k)
        mask = (j * LANE + lane) < qpos
        soft = _log1p_exp_neg_abs(z)
        log_beta = jnp.minimum(z, 0.0) - soft
        log_fail = jnp.where(mask, jnp.minimum(-z, 0.0) - soft, 0.0)
        hi = log_fail.astype(bf16)
        lo = (log_fail - hi.astype(f32)).astype(bf16)
        tri = tri_ref[...]
        ext = _dot(hi, tri) + _dot(lo, tri)
        carry = carry_sc[...]
        a = jnp.where(mask, jnp.exp(log_beta + ext[:, :LANE] + carry), 0.0)
        acc_sc[...] += _dot(a.astype(bf16), v)
        carry = carry + ext[:, LANE:]
        carry_sc[...] = carry
        return j - 1, (jnp.max(carry) > SB_DEAD).astype(i32)

    lax.while_loop(cond, body, (qi, jnp.int32(1)))

    acc = acc_sc[...]
    lo = _lane_iota((tq, LANE)) < 64
    for p in range(heads // 2):
        e = acc[(2 * p) * tq:(2 * p + 1) * tq]
        o = acc[(2 * p + 1) * tq:(2 * p + 2) * tq]
        if p < 2:
            y = jnp.where(lo, e, pltpu.roll(o, 64, 1))
        else:
            y = jnp.where(lo, pltpu.roll(e, 64, 1), o)
        o_ref[:, p * LANE:(p + 1) * LANE] = y.astype(o_ref.dtype)


def _mla_prompt_kernel(q_ref, rows_ref, wuv_ref, o_ref, m_sc, l_sc, acc_sc, *, tq):
    qi = pl.program_id(1)
    heads = MLA_HEADS
    rows = heads * tq
    tk = tq
    q = jnp.concatenate([q_ref[:, h * MLA_ROW:(h + 1) * MLA_ROW] for h in range(heads)], axis=0)
    qpos = qi * tq + lax.broadcasted_iota(i32, (rows, tk), 0) % tq
    lane = _lane_iota((rows, tk))
    m_sc[...] = jnp.full_like(m_sc, NEG_BIG)
    l_sc[...] = jnp.zeros_like(l_sc)
    acc_sc[...] = jnp.zeros_like(acc_sc)

    def body(j, _):
        off = pl.multiple_of(j * tk, tk)
        k = rows_ref[pl.ds(off, tk), :]
        z = _dot_nt(q, k)
        z = jnp.where((j * tk + lane) <= qpos, z, NEG_BIG)
        m_prev = m_sc[...]
        m_new = jnp.maximum(m_prev, jnp.max(z, axis=1, keepdims=True))
        alpha = jnp.exp(m_prev - m_new)
        p = jnp.exp(z - m_new)
        l_sc[...] = alpha * l_sc[...] + jnp.sum(p, axis=1, keepdims=True)
        acc_sc[...] = alpha * acc_sc[...] + _dot(p.astype(bf16), k)
        m_sc[...] = m_new
        return 0

    lax.fori_loop(0, qi + 1, body, 0)
    o = acc_sc[...] / l_sc[...]
    ocat = jnp.concatenate([o[h * tq:(h + 1) * tq] for h in range(heads)], axis=1).astype(bf16)
    o_ref[...] = _dot(ocat, wuv_ref[...]).astype(o_ref.dtype)


def _count_ge(key_ref, cand, n_blk, tk, ones):
    def blk(jb, cnt):
        off = pl.multiple_of(jb * tk, tk)
        kk = key_ref[:, pl.ds(off, tk)]
        for t in range(tk // LANE):
            cnt = cnt + jnp.where(kk[:, t * LANE:(t + 1) * LANE] >= cand, 1.0, 0.0)
        return cnt
    cnt = lax.fori_loop(0, n_blk, blk, jnp.zeros(cand.shape, f32))
    return _dot(cnt.astype(bf16), ones)


def _nth_largest_key(key_ref, n_blk, tk, ones, n_sel, rows):
    def bit(i, t):
        cand = t + (jnp.int32(1) << (31 - i))
        tot = _count_ge(key_ref, cand, n_blk, tk, ones)
        return jnp.where(tot >= n_sel, cand, t)
    return lax.fori_loop(0, 32, bit, jnp.full((rows, LANE), INT_MIN, i32))


def _dsa_prompt_kernel(qd_ref, qi_ref, w_ref, kv_ref, ik_ref, tri_ref, ones_ref, o_ref,
                       key_sc, wb_sc, tie_sc, m_sc, l_sc, acc_sc, *, tq, n_sel):
    qb = pl.program_id(1)
    tk = tq
    n_blk = qb + 1
    lanes_per_blk = tk // LANE
    ones = ones_ref[...]
    row_pos = qb * tq + lax.broadcasted_iota(i32, (tq, tk), 0)
    lane = _lane_iota((tq, tk))

    w = w_ref[...]
    for h in range(IDX_HEADS):
        wb_sc[h] = jnp.broadcast_to(w[:, 64 + h:65 + h], (tq, LANE))

    grp = 4

    def score_blk(j, _):
        off = pl.multiple_of(j * tk, tk)
        ik = ik_ref[pl.ds(off, tk), :]
        score = jnp.zeros((tq, tk), f32)
        for g in range(IDX_HEADS // grp):
            qg = jnp.concatenate(
                [qi_ref[:, (grp * g + i) * LANE:(grp * g + i + 1) * LANE] for i in range(grp)], axis=0)
            s = _dot_nt(qg, ik)
            for i in range(grp):
                wbh = wb_sc[grp * g + i]
                wfull = jnp.concatenate([wbh] * lanes_per_blk, axis=1)
                score = score + jnp.maximum(s[i * tq:(i + 1) * tq], 0.0) * wfull
        score = score + 0.0
        score = jnp.where((j * tk + lane) <= row_pos, score, -jnp.inf)
        key_sc[:, pl.ds(off, tk)] = _score_key(score)
        return 0

    lax.fori_loop(0, n_blk, score_blk, 0)

    thr = _nth_largest_key(key_sc, n_blk, tk, ones, n_sel, tq)
    need = n_sel - _count_ge(key_sc, thr + 1, n_blk, tk, ones)
    thr_f = jnp.concatenate([thr] * lanes_per_blk, axis=1)
    need_f = jnp.concatenate([need] * lanes_per_blk, axis=1)

    heads = DSA_HEADS
    m_sc[...] = jnp.full_like(m_sc, NEG_BIG)
    l_sc[...] = jnp.zeros_like(l_sc)
    acc_sc[...] = jnp.zeros_like(acc_sc)
    tie_sc[...] = jnp.zeros_like(tie_sc)
    q = jnp.concatenate([qd_ref[:, h * LANE:(h + 1) * LANE] for h in range(heads)], axis=0)

    def att_blk(j, _):
        off = pl.multiple_of(j * tk, tk)
        kk = key_sc[:, pl.ds(off, tk)]
        eq = kk == thr_f
        pre = _dot(jnp.where(eq, 1.0, 0.0).astype(bf16), tri_ref[...])
        tie = tie_sc[...]
        rank = pre[:, :tk] + jnp.concatenate([tie] * lanes_per_blk, axis=1)
        take = jnp.logical_or(kk > thr_f, jnp.logical_and(eq, rank < need_f))
        valid = jnp.logical_and(take, (j * tk + lane) <= row_pos)
        tie_sc[...] = tie + pre[:, tk:]
        kvb = kv_ref[pl.ds(off, tk), :]
        z = _dot_nt(q, kvb).reshape(heads, tq, tk)
        valid3 = jnp.broadcast_to(valid[None], (heads, tq, tk))
        z = jnp.where(valid3, z, NEG_BIG)
        m_prev = m_sc[...]
        m_new = jnp.maximum(m_prev, jnp.max(z, axis=2, keepdims=True))
        alpha = jnp.exp(m_prev - m_new)
        p = jnp.where(valid3, jnp.exp(z - m_new), 0.0)
        l_sc[...] = alpha * l_sc[...] + jnp.sum(p, axis=2, keepdims=True)
        pv = _dot(p.reshape(heads * tq, tk).astype(bf16), kvb).reshape(heads, tq, LANE)
        acc_sc[...] = alpha * acc_sc[...] + pv
        m_sc[...] = m_new
        return 0

    lax.fori_loop(0, n_blk, att_blk, 0)
    o = acc_sc[...] / l_sc[...]
    lo = _lane_iota((tq, LANE)) < 64
    for p in range(heads // 2):
        y = jnp.where(lo, pltpu.roll(o[2 * p], 64, 1), o[2 * p + 1])
        o_ref[:, p * LANE:(p + 1) * LANE] = y.astype(o_ref.dtype)


def _page_copy(cache_ref, layer, page, dst, sem):
    return pltpu.make_async_copy(cache_ref.at[layer, page], dst, sem)


def _sb_sample_kernel(pt_ref, q_ref, knew_ref, vnew_ref, ck_ref, cv_ref, tri_ref, o_ref,
                      kbuf, vbuf, sem_k, sem_v, carry_sc, acc_sc, *, layer, n_pages):
    b = pl.program_id(0)
    heads = SB_HEADS
    q = q_ref[0]

    def copies(j, slot):
        page = pt_ref[b, j]
        return (_page_copy(ck_ref, layer, page, kbuf.at[slot], sem_k.at[slot]),
                _page_copy(cv_ref, layer, page, vbuf.at[slot], sem_v.at[slot]))

    def start(j, slot):
        for c in copies(j, slot):
            c.start()

    def wait(slot):
        for c in copies(0, slot):
            c.wait()

    start(n_pages - 1, 0)

    q_pos = n_pages * PAGE_SIZE
    new_pos = n_pages * PAGE_SIZE
    z_new = jnp.sum(q.astype(f32) * knew_ref[0].astype(bf16).astype(f32), axis=1, keepdims=True)
    new_ok = jnp.full((heads, 1), new_pos < q_pos)
    soft_new = _log1p_exp_neg_abs(z_new)
    a_new = jnp.where(new_ok, jnp.exp(jnp.minimum(z_new, 0.0) - soft_new), 0.0)
    acc_sc[...] = a_new.astype(bf16).astype(f32) * vnew_ref[0].astype(bf16).astype(f32)
    carry_sc[...] = jnp.broadcast_to(jnp.where(new_ok, jnp.minimum(-z_new, 0.0) - soft_new, 0.0),
                                     (heads, LANE))

    def cond(s):
        j, alive = s
        return jnp.logical_and(j >= 0, alive > 0)

    def body(s):
        j, _ = s
        slot = (n_pages - 1 - j) & 1
        wait(slot)

        @pl.when(j > 0)
        def _():
            start(j - 1, 1 - slot)

        k = kbuf[slot].astype(bf16)
        v = vbuf[slot].astype(bf16)
        z = _dot_nt(q, k)
        soft = _log1p_exp_neg_abs(z)
        log_beta = jnp.minimum(z, 0.0) - soft
        log_fail = jnp.minimum(-z, 0.0) - soft
        hi = log_fail.astype(bf16)
        lo = (log_fail - hi.astype(f32)).astype(bf16)
        tri = tri_ref[...]
        ext = _dot(hi, tri) + _dot(lo, tri)
        carry = carry_sc[...]
        a = jnp.exp(log_beta + ext[:, :LANE] + carry)
        acc_sc[...] += _dot(a.astype(bf16), v)
        carry = carry + ext[:, LANE:]
        carry_sc[...] = carry
        return j - 1, (jnp.max(carry) > SB_DEAD).astype(i32)

    j_end, _ = lax.while_loop(cond, body, (jnp.int32(n_pages - 1), jnp.int32(1)))

    @pl.when(j_end >= 0)
    def _():
        wait((n_pages - 1 - j_end) & 1)

    o_ref[0] = acc_sc[...]


def _mla_sample_kernel(pt_ref, q_ref, rnew_ref, cache_ref, o_ref, buf, sem, *, layer, n_pages, chunk):
    b = pl.program_id(0)
    n_chunks = n_pages // chunk
    q = q_ref[0]
    width = MLA_KV_LORA + MLA_ROPE

    def copies(c, slot):
        return [_page_copy(cache_ref, layer, pt_ref[b, c * chunk + i], buf.at[slot, i], sem.at[slot])
                for i in range(chunk)]

    def start(c, slot):
        for cp in copies(c, slot):
            cp.start()

    def wait(slot):
        for cp in copies(0, slot):
            cp.wait()

    start(0, 0)
    qk = q[:, :width]
    rnew = rnew_ref[0].astype(bf16)
    m0 = jnp.sum(q.astype(f32) * rnew.astype(f32), axis=1, keepdims=True)
    l0 = jnp.ones_like(m0)
    acc0 = jnp.broadcast_to(rnew.astype(f32)[:, :width], (MLA_HEADS, width))

    def body(c, s):
        m_prev, l_prev, acc = s
        slot = c & 1
        wait(slot)

        @pl.when(c + 1 < n_chunks)
        def _():
            start(c + 1, 1 - slot)

        k = buf[slot].reshape(chunk * PAGE_SIZE, width).astype(bf16)
        z = _dot_nt(qk, k)
        m_new = jnp.maximum(m_prev, jnp.max(z, axis=1, keepdims=True))
        alpha = jnp.exp(m_prev - m_new)
        p = jnp.exp(z - m_new)
        l_new = alpha * l_prev + jnp.sum(p, axis=1, keepdims=True)
        acc = alpha * acc + _dot(p.astype(bf16), k)
        return m_new, l_new, acc

    _, l_fin, acc = lax.fori_loop(0, n_chunks, body, (m0, l0, acc0))
    o = acc / l_fin
    o_ref[0] = jnp.concatenate([o, jnp.zeros((MLA_HEADS, MLA_ROW - width), f32)], axis=1).astype(o_ref.dtype)


def _idx_sample_kernel(pt_ref, qi_ref, w_ref, iknew_ref, cache_ref, o_ref, buf, sem,
                       *, layer, n_pages, chunk):
    b = pl.program_id(0)
    n_chunks = n_pages // chunk
    q = qi_ref[0]
    qk = q[:, :IDX_DIM]
    w = w_ref[0]
    span = chunk * PAGE_SIZE

    def copies(c, slot):
        return [_page_copy(cache_ref, layer, pt_ref[b, c * chunk + i], buf.at[slot, i], sem.at[slot])
                for i in range(chunk)]

    def start(c, slot):
        for cp in copies(c, slot):
            cp.start()

    def wait(slot):
        for cp in copies(0, slot):
            cp.wait()

    start(0, 0)

    def body(c, _):
        slot = c & 1
        wait(slot)

        @pl.when(c + 1 < n_chunks)
        def _():
            start(c + 1, 1 - slot)

        k = buf[slot].reshape(span, IDX_DIM).astype(bf16)
        s = _dot_nt(qk, k)
        score = jnp.sum(jnp.maximum(s, 0.0) * w, axis=0, keepdims=True) + 0.0
        o_ref[0, :, pl.ds(pl.multiple_of(c * span, span), span)] = score
        return 0

    lax.fori_loop(0, n_chunks, body, 0)
    iknew = iknew_ref[0].astype(bf16).astype(f32)
    s_new = jnp.sum(q.astype(f32) * iknew, axis=1, keepdims=True)
    score_new = jnp.sum(jnp.maximum(s_new, 0.0) * w, axis=0, keepdims=True) + 0.0
    lane = _lane_iota((1, LANE))
    o_ref[0, :, pl.ds(n_pages * PAGE_SIZE, LANE)] = jnp.where(lane == 0, score_new, -jnp.inf)


def _select_sample_kernel(score_ref, tri_ref, ones_ref, sel_ref, key_sc, *, n_sel, n_tiles):
    rows = score_ref.shape[0]
    ones = ones_ref[...]

    def to_key(t, _):
        off = pl.multiple_of(t * LANE, LANE)
        key_sc[:, pl.ds(off, LANE)] = _score_key(score_ref[:, pl.ds(off, LANE)])
        return 0

    lax.fori_loop(0, n_tiles, to_key, 0)
    thr = _nth_largest_key(key_sc, n_tiles, LANE, ones, n_sel, rows)
    need = n_sel - _count_ge(key_sc, thr + 1, n_tiles, LANE, ones)

    def sel_tile(t, tie):
        off = pl.multiple_of(t * LANE, LANE)
        kk = key_sc[:, pl.ds(off, LANE)]
        eq = kk == thr
        pre = _dot(jnp.where(eq, 1.0, 0.0).astype(bf16), tri_ref[...])
        take = jnp.logical_or(kk > thr, jnp.logical_and(eq, (pre[:, :LANE] + tie) < need))
        sel_ref[:, pl.ds(off, LANE)] = jnp.where(take, 1.0, 0.0)
        return tie + pre[:, LANE:]

    lax.fori_loop(0, n_tiles, sel_tile, jnp.zeros((rows, LANE), f32))


def _dsa_sample_kernel(pt_ref, q_ref, sel_ref, kvnew_ref, ck_ref, cv_ref, o_ref,
                       kbuf, vbuf, sem_k, sem_v, *, layer, n_pages, chunk):
    b = pl.program_id(0)
    n_chunks = n_pages // chunk
    heads = DSA_HEADS
    q = q_ref[0]
    qk = q[:, :HEAD_DIM]
    span = chunk * PAGE_SIZE

    def copies(c, slot):
        out = []
        for i in range(chunk):
            page = pt_ref[b, c * chunk + i]
            out.append(_page_copy(ck_ref, layer, page, kbuf.at[slot, i], sem_k.at[slot]))
            out.append(_page_copy(cv_ref, layer, page, vbuf.at[slot, i], sem_v.at[slot]))
        return out

    def start(c, slot):
        for cp in copies(c, slot):
            cp.start()

    def wait(slot):
        for cp in copies(0, slot):
            cp.wait()

    start(0, 0)
    kvnew = kvnew_ref[0].astype(bf16).astype(f32)
    z_new = jnp.sum(q.astype(f32) * kvnew, axis=1, keepdims=True)
    new_ok = sel_ref[0, :, pl.ds(n_pages * PAGE_SIZE, LANE)][:, :1] > 0.5
    m0 = jnp.where(new_ok, z_new, NEG_BIG)
    l0 = jnp.where(new_ok, jnp.ones_like(z_new), 0.0)
    acc0 = jnp.where(new_ok, jnp.broadcast_to(kvnew[:, HEAD_DIM:], (heads, HEAD_DIM)), 0.0)

    def body(c, s):
        m_prev, l_prev, acc = s
        slot = c & 1
        wait(slot)

        @pl.when(c + 1 < n_chunks)
        def _():
            start(c + 1, 1 - slot)

        k = kbuf[slot].reshape(span, HEAD_DIM).astype(bf16)
        v = vbuf[slot].reshape(span, HEAD_DIM).astype(bf16)
        valid = sel_ref[0, :, pl.ds(pl.multiple_of(c * span, span), span)] > 0.5
        z = jnp.where(valid, _dot_nt(qk, k), NEG_BIG)
        m_new = jnp.maximum(m_prev, jnp.max(z, axis=1, keepdims=True))
        alpha = jnp.exp(m_prev - m_new)
        p = jnp.where(valid, jnp.exp(z - m_new), 0.0)
        l_new = alpha * l_prev + jnp.sum(p, axis=1, keepdims=True)
        acc = alpha * acc + _dot(p.astype(bf16), v)
        return m_new, l_new, acc

    _, l_fin, acc = lax.fori_loop(0, n_chunks, body, (m0, l0, acc0))
    o_ref[0] = acc / l_fin


def _rope_tables(pos, dim):
    half = dim // 2
    inv_freq = ROPE_THETA ** (-jnp.arange(half, dtype=f32) / half)
    ang = pos.astype(f32)[:, None] * inv_freq[None, :]
    cos, sin = jnp.cos(ang), jnp.sin(ang)
    reps = LANE // dim
    return (jnp.tile(jnp.concatenate([cos, cos], axis=-1), (1, reps)),
            jnp.tile(jnp.concatenate([-sin, sin], axis=-1), (1, reps)))


def _layer_weights(l, w_in, mla_q_norm, mla_kv_norm, mla_w_uq, mla_w_uk, mla_w_uv,
                   w_branch_sb, w_branch_mla, w_branch_dsa, w_out, w_up, w_down):
    w = w_in[l].astype(bf16)
    c = 0
    cols = {}
    for name, n in (("sb", 768), ("mla_cq", 256), ("mla_kv", 224), ("dsa_q", 512), ("dsa_kv", 128),
                    ("idx_q", 1024), ("idx_kw", 80), ("gates", 3 * D_MODEL)):
        cols[name] = w[:, c:c + n]
        c += n
    pad = lambda a, n: jnp.pad(a, ((0, 0), (0, n - a.shape[1])))
    w_mla = jnp.concatenate([cols["mla_cq"], pad(cols["mla_kv"], MLA_ROW)], axis=1)
    w_dsa = jnp.concatenate([cols["dsa_q"], cols["dsa_kv"], cols["idx_q"], pad(cols["idx_kw"], LANE)], axis=1)
    uq = mla_w_uq[l].astype(bf16)
    wuq = jnp.concatenate([uq[:, :, :MLA_NOPE].reshape(MLA_Q_LORA, MLA_HEADS * MLA_NOPE),
                           uq[:, :, MLA_NOPE:].reshape(MLA_Q_LORA, MLA_HEADS * MLA_ROPE)], axis=1)
    uk = mla_w_uk[l].astype(bf16)
    uv = mla_w_uv[l].astype(bf16)
    wabs = jnp.zeros((MLA_HEADS * MLA_NOPE, MLA_HEADS * MLA_ROW), bf16)
    wuv = jnp.zeros((MLA_HEADS * MLA_ROW, MLA_HEADS * MLA_V), bf16)
    for h in range(MLA_HEADS):
        wabs = wabs.at[h * MLA_NOPE:(h + 1) * MLA_NOPE, h * MLA_ROW:h * MLA_ROW + MLA_KV_LORA].set(uk[:, h, :].T)
        wuv = wuv.at[h * MLA_ROW:h * MLA_ROW + MLA_KV_LORA, h * MLA_V:(h + 1) * MLA_V].set(uv[:, h, :])
    return dict(
        w_sb=cols["sb"], w_mla=w_mla, w_dsa=w_dsa, w_gates=cols["gates"],
        qn=mla_q_norm[l][None, :], kvn=jnp.pad(mla_kv_norm[l], (0, MLA_ROW - MLA_KV_LORA))[None, :],
        wuq=wuq, wabs=wabs, wuv=wuv,
        w_bsb=w_branch_sb[l].astype(bf16), w_bmla=w_branch_mla[l].astype(bf16),
        w_bdsa=w_branch_dsa[l].astype(bf16), w_out=w_out[l].astype(bf16),
        w_up=w_up[l].astype(bf16), w_down=w_down[l].astype(bf16))


def _rope_place_matrix():
    src = jnp.arange(MLA_HEADS * MLA_ROPE)
    dst = (src // MLA_ROPE) * MLA_ROW + MLA_KV_LORA + src % MLA_ROPE
    return jnp.zeros((MLA_HEADS * MLA_ROPE, MLA_HEADS * MLA_ROW), bf16).at[src, dst].set(1)


def _suffix_matrix(n):
    j = jnp.arange(n)[:, None]
    s = jnp.arange(n)[None, :]
    return jnp.concatenate([(j > s), jnp.ones((n, LANE), bool)], axis=1).astype(bf16)


def _prefix_matrix(n):
    j = jnp.arange(n)[:, None]
    s = jnp.arange(n)[None, :]
    return jnp.concatenate([(j < s), jnp.ones((n, LANE), bool)], axis=1).astype(bf16)


def _row_call(kernel, n_rows, tm, row_ins, const_ins, pos_ins, outs):
    in_specs = [pl.BlockSpec((tm, a.shape[1]), lambda i: (i, 0)) for a in row_ins]
    in_specs += [_full(a.shape) for a in const_ins]
    for a, nb in pos_ins:
        in_specs.append(pl.BlockSpec((tm, a.shape[1]), functools.partial(lambda i, nb: (i % nb, 0), nb=nb)))
    return pl.pallas_call(
        kernel,
        grid=(n_rows // tm,),
        in_specs=in_specs,
        out_specs=[pl.BlockSpec((tm, c), lambda i: (i, 0)) for c, _ in outs],
        out_shape=[jax.ShapeDtypeStruct((n_rows, c), d) for c, d in outs],
        compiler_params=_params(1),
    )(*row_ins, *const_ins, *[a for a, _ in pos_ins])


def _project(x, g, lw, tabs, tm, place):
    n = x.shape[0]
    cos64, sin64, cos32, sin32, nb = tabs
    q_sb, k_sb, v_sb = _row_call(
        _proj_sb_kernel, n, tm, [x], [g, lw["w_sb"]], [],
        [(SB_HEADS * LANE, bf16), (LANE, f32), (LANE, f32)])
    q_mla, row, rowb = _row_call(
        _proj_mla_kernel, n, tm, [x],
        [g, lw["w_mla"], lw["qn"], lw["kvn"], lw["wuq"], lw["wabs"], place],
        [(cos32, nb), (sin32, nb)],
        [(MLA_HEADS * MLA_ROW, bf16), (MLA_ROW, f32), (MLA_ROW, bf16)])
    q_dsa, kv, kvb, q_idx, ikw, ikwb = _row_call(
        _proj_dsa_kernel, n, tm, [x], [g, lw["w_dsa"]], [(cos64, nb), (sin64, nb)],
        [(DSA_HEADS * LANE, bf16), (LANE, f32), (LANE, bf16),
         (IDX_HEADS * LANE, bf16), (LANE, f32), (LANE, bf16)])
    return dict(q_sb=q_sb, k_sb=k_sb, v_sb=v_sb, q_mla=q_mla, row=row, rowb=rowb,
                q_dsa=q_dsa, kv=kv, kvb=kvb, q_idx=q_idx, ikw=ikw, ikwb=ikwb)


def _merge_mlp(x, y_sb, y_mla, y_dsa, g_mix, g_mlp, g_final, lw, tm, final_norm):
    n = x.shape[0]
    x1 = pl.pallas_call(
        _merge_kernel,
        grid=(n // tm,),
        in_specs=[pl.BlockSpec((tm, D_MODEL), lambda i: (i, 0)), _full(g_mix.shape), _full(lw["w_gates"].shape),
                  pl.BlockSpec((tm, 512), lambda i: (i, 0)), pl.BlockSpec((tm, 512), lambda i: (i, 0)),
                  pl.BlockSpec((tm, 512), lambda i: (i, 0)),
                  _full(lw["w_bsb"].shape), _full(lw["w_bmla"].shape), _full(lw["w_bdsa"].shape),
                  _full(lw["w_out"].shape)],
        out_specs=pl.BlockSpec((tm, D_MODEL), lambda i: (i, 0)),
        out_shape=jax.ShapeDtypeStruct((n, D_MODEL), f32),
        compiler_params=_params(1),
    )(x, g_mix, lw["w_gates"], y_sb, y_mla, y_dsa, lw["w_bsb"], lw["w_bmla"], lw["w_bdsa"], lw["w_out"])
    return pl.pallas_call(
        functools.partial(_mlp_kernel, final_norm=final_norm),
        grid=(n // tm,),
        in_specs=[pl.BlockSpec((tm, D_MODEL), lambda i: (i, 0)), _full(g_mlp.shape),
                  _full(lw["w_up"].shape), _full(lw["w_down"].shape), _full(g_final.shape)],
        out_specs=pl.BlockSpec((tm, D_MODEL), lambda i: (i, 0)),
        out_shape=jax.ShapeDtypeStruct((n, D_MODEL), f32),
        compiler_params=_params(1),
    )(x1, g_mlp, lw["w_up"], lw["w_down"], g_final)


def _prompt_attention(pr, lw, batch, lp, n_sel):
    n = batch * lp
    nq = lp // SB_TQ
    y_sb = pl.pallas_call(
        functools.partial(_sb_prompt_kernel, tq=SB_TQ),
        grid=(batch, nq),
        in_specs=[pl.BlockSpec((SB_TQ, SB_HEADS * LANE), lambda b, i: (b * nq + i, 0)),
                  pl.BlockSpec((lp, LANE), lambda b, i: (b, 0)),
                  pl.BlockSpec((lp, LANE), lambda b, i: (b, 0)),
                  _full((LANE, 2 * LANE))],
        out_specs=pl.BlockSpec((SB_TQ, 512), lambda b, i: (b * nq + i, 0)),
        out_shape=jax.ShapeDtypeStruct((n, 512), bf16),
        scratch_shapes=[pltpu.VMEM((SB_HEADS * SB_TQ, LANE), f32), pltpu.VMEM((SB_HEADS * SB_TQ, LANE), f32)],
        compiler_params=_params(2),
    )(pr["q_sb"], pr["k_sb"], pr["v_sb"], _suffix_matrix(LANE))
    tq = ATT_TQ
    nq = lp // tq
    y_mla = pl.pallas_call(
        functools.partial(_mla_prompt_kernel, tq=tq),
        grid=(batch, nq),
        in_specs=[pl.BlockSpec((tq, MLA_HEADS * MLA_ROW), lambda b, i: (b * nq + i, 0)),
                  pl.BlockSpec((lp, MLA_ROW), lambda b, i: (b, 0)),
                  _full(lw["wuv"].shape)],
        out_specs=pl.BlockSpec((tq, 512), lambda b, i: (b * nq + i, 0)),
        out_shape=jax.ShapeDtypeStruct((n, 512), bf16),
        scratch_shapes=[pltpu.VMEM((MLA_HEADS * tq, 1), f32), pltpu.VMEM((MLA_HEADS * tq, 1), f32),
                        pltpu.VMEM((MLA_HEADS * tq, MLA_ROW), f32)],
        compiler_params=_params(2),
    )(pr["q_mla"], pr["rowb"], lw["wuv"])
    y_dsa = pl.pallas_call(
        functools.partial(_dsa_prompt_kernel, tq=tq, n_sel=n_sel),
        grid=(batch, nq),
        in_specs=[pl.BlockSpec((tq, DSA_HEADS * LANE), lambda b, i: (b * nq + i, 0)),
                  pl.BlockSpec((tq, IDX_HEADS * LANE), lambda b, i: (b * nq + i, 0)),
                  pl.BlockSpec((tq, LANE), lambda b, i: (b * nq + i, 0)),
                  pl.BlockSpec((lp, LANE), lambda b, i: (b, 0)),
                  pl.BlockSpec((lp, LANE), lambda b, i: (b, 0)),
                  _full((tq, tq + LANE)), _full((LANE, LANE))],
        out_specs=pl.BlockSpec((tq, 512), lambda b, i: (b * nq + i, 0)),
        out_shape=jax.ShapeDtypeStruct((n, 512), bf16),
        scratch_shapes=[pltpu.VMEM((tq, lp), i32), pltpu.VMEM((IDX_HEADS, tq, LANE), f32),
                        pltpu.VMEM((tq, LANE), f32),
                        pltpu.VMEM((DSA_HEADS, tq, 1), f32), pltpu.VMEM((DSA_HEADS, tq, 1), f32),
                        pltpu.VMEM((DSA_HEADS, tq, LANE), f32)],
        compiler_params=_params(2),
    )(pr["q_dsa"], pr["q_idx"], pr["ikw"], pr["kvb"], pr["ikwb"], _prefix_matrix(tq),
      jnp.ones((LANE, LANE), bf16))
    return y_sb, y_mla, y_dsa


def _paged_spec(block, index_map):
    return pl.BlockSpec(block, index_map)


def _sample_attention(pr, lw, layer, caches, page_table, n_sel):
    cache_sb_k, cache_sb_v, cache_mla, cache_dsa_k, cache_dsa_v, cache_dsa_idx = caches
    db, n_pages = page_table.shape
    p_len = n_pages * PAGE_SIZE
    chunk = 8 if n_pages % 8 == 0 else n_pages
    any_spec = pl.BlockSpec(memory_space=pl.ANY)
    per_b3 = lambda shape: pl.BlockSpec((1,) + shape, lambda b, pt: (b, 0, 0))

    ck = cache_sb_k.reshape(cache_sb_k.shape[:3] + (LANE,))
    cv = cache_sb_v.reshape(cache_sb_v.shape[:3] + (LANE,))
    o_sb = pl.pallas_call(
        functools.partial(_sb_sample_kernel, layer=layer, n_pages=n_pages),
        grid_spec=pltpu.PrefetchScalarGridSpec(
            num_scalar_prefetch=1, grid=(db,),
            in_specs=[per_b3((SB_HEADS, LANE)), per_b3((1, LANE)), per_b3((1, LANE)), any_spec, any_spec,
                      pl.BlockSpec((LANE, 2 * LANE), lambda b, pt: (0, 0))],
            out_specs=per_b3((SB_HEADS, LANE)),
            scratch_shapes=[pltpu.VMEM((2, PAGE_SIZE, LANE), f32), pltpu.VMEM((2, PAGE_SIZE, LANE), f32),
                            pltpu.SemaphoreType.DMA((2,)), pltpu.SemaphoreType.DMA((2,)),
                            pltpu.VMEM((SB_HEADS, LANE), f32), pltpu.VMEM((SB_HEADS, LANE), f32)]),
        out_shape=jax.ShapeDtypeStruct((db, SB_HEADS, LANE), f32),
        compiler_params=_params(1),
    )(page_table, pr["q_sb"].reshape(db, SB_HEADS, LANE), pr["k_sb"].reshape(db, 1, LANE),
      pr["v_sb"].reshape(db, 1, LANE), ck, cv, _suffix_matrix(LANE))
    half = jnp.arange(SB_HEADS) // (SB_HEADS // SB_KV_HEADS)
    y_sb = jnp.where(half[None, :, None] == 0, o_sb[:, :, :HEAD_DIM], o_sb[:, :, HEAD_DIM:])
    y_sb = y_sb.reshape(db, SB_HEADS * HEAD_DIM).astype(bf16)

    width = MLA_KV_LORA + MLA_ROPE
    o_mla = pl.pallas_call(
        functools.partial(_mla_sample_kernel, layer=layer, n_pages=n_pages, chunk=chunk),
        grid_spec=pltpu.PrefetchScalarGridSpec(
            num_scalar_prefetch=1, grid=(db,),
            in_specs=[per_b3((MLA_HEADS, MLA_ROW)), per_b3((1, MLA_ROW)), any_spec],
            out_specs=per_b3((MLA_HEADS, MLA_ROW)),
            scratch_shapes=[pltpu.VMEM((2, chunk, PAGE_SIZE, width), f32), pltpu.SemaphoreType.DMA((2,))]),
        out_shape=jax.ShapeDtypeStruct((db, MLA_HEADS, MLA_ROW), bf16),
        compiler_params=_params(1),
    )(page_table, pr["q_mla"].reshape(db, MLA_HEADS, MLA_ROW), pr["row"].reshape(db, 1, MLA_ROW), cache_mla)
    y_mla = pl.pallas_call(
        _matmul_kernel,
        in_specs=[_full((db, MLA_HEADS * MLA_ROW)), _full(lw["wuv"].shape)],
        out_specs=_full((db, 512)),
        out_shape=jax.ShapeDtypeStruct((db, 512), bf16),
        grid=(1,),
        compiler_params=_params(1),
    )(o_mla.reshape(db, MLA_HEADS * MLA_ROW), lw["wuv"])

    kp = p_len + LANE
    w_col = pr["ikw"][:, 64:64 + IDX_HEADS].reshape(db, IDX_HEADS, 1)
    scores = pl.pallas_call(
        functools.partial(_idx_sample_kernel, layer=layer, n_pages=n_pages, chunk=chunk),
        grid_spec=pltpu.PrefetchScalarGridSpec(
            num_scalar_prefetch=1, grid=(db,),
            in_specs=[per_b3((IDX_HEADS, LANE)), per_b3((IDX_HEADS, 1)), per_b3((1, LANE)), any_spec],
            out_specs=per_b3((1, kp)),
            scratch_shapes=[pltpu.VMEM((2, chunk, PAGE_SIZE, IDX_DIM), f32), pltpu.SemaphoreType.DMA((2,))]),
        out_shape=jax.ShapeDtypeStruct((db, 1, kp), f32),
        compiler_params=_params(1),
    )(page_table, pr["q_idx"].reshape(db, IDX_HEADS, LANE), w_col, pr["ikw"].reshape(db, 1, LANE),
      cache_dsa_idx)
    rb = 32 if db % 32 == 0 else db
    sel = pl.pallas_call(
        functools.partial(_select_sample_kernel, n_sel=n_sel, n_tiles=kp // LANE),
        grid=(db // rb,),
        in_specs=[pl.BlockSpec((rb, kp), lambda i: (i, 0)), _full((LANE, 2 * LANE)), _full((LANE, LANE))],
        out_specs=pl.BlockSpec((rb, kp), lambda i: (i, 0)),
        out_shape=jax.ShapeDtypeStruct((db, kp), f32),
        scratch_shapes=[pltpu.VMEM((rb, kp), i32)],
        compiler_params=_params(1),
    )(scores.reshape(db, kp), _prefix_matrix(LANE), jnp.ones((LANE, LANE), bf16))
    o_dsa = pl.pallas_call(
        functools.partial(_dsa_sample_kernel, layer=layer, n_pages=n_pages, chunk=chunk),
        grid_spec=pltpu.PrefetchScalarGridSpec(
            num_scalar_prefetch=1, grid=(db,),
            in_specs=[per_b3((DSA_HEADS, LANE)), per_b3((1, kp)), per_b3((1, LANE)), any_spec, any_spec],
            out_specs=per_b3((DSA_HEADS, HEAD_DIM)),
            scratch_shapes=[pltpu.VMEM((2, chunk, PAGE_SIZE, HEAD_DIM), f32),
                            pltpu.VMEM((2, chunk, PAGE_SIZE, HEAD_DIM), f32),
                            pltpu.SemaphoreType.DMA((2,)), pltpu.SemaphoreType.DMA((2,))]),
        out_shape=jax.ShapeDtypeStruct((db, DSA_HEADS, HEAD_DIM), f32),
        compiler_params=_params(1),
    )(page_table, pr["q_dsa"].reshape(db, DSA_HEADS, LANE), sel.reshape(db, 1, kp),
      pr["kv"].reshape(db, 1, LANE), cache_dsa_k, cache_dsa_v)
    y_dsa = o_dsa.reshape(db, DSA_HEADS * HEAD_DIM).astype(bf16)
    return y_sb, y_mla, y_dsa


def kernel(x_prompt, x_sample, cache_sb_k, cache_sb_v, cache_mla, cache_dsa_k, cache_dsa_v, cache_dsa_idx, page_table, meta, norm_mix, w_in, mla_q_norm, mla_kv_norm, mla_w_uq, mla_w_uk, mla_w_uv, w_branch_sb, w_branch_mla, w_branch_dsa, w_out, norm_mlp, w_up, w_down, norm_final):
    batch, seq, _ = x_prompt.shape
    db, ds, _ = x_sample.shape
    assert ds == 1, "one new token per decode entry"
    depth = w_in.shape[0]
    n_pages = page_table.shape[1]
    p_len = n_pages * PAGE_SIZE
    t_len = seq + N_META
    lp = -(-t_len // ATT_TQ) * ATT_TQ
    n_sel_p = min(TOPK_MAX, seq // 4)
    n_sel_s = min(TOPK_MAX, (p_len + ds) // 4)

    xp = jnp.concatenate([jnp.broadcast_to(meta.astype(f32)[None], (batch, N_META, D_MODEL)), x_prompt], axis=1)
    xp = jnp.pad(xp, ((0, 0), (0, lp - t_len), (0, 0))).reshape(batch * lp, D_MODEL)
    xs = x_sample.reshape(db * ds, D_MODEL)

    pos_p = jnp.arange(lp)
    pos_s = jnp.full((db,), p_len, jnp.int32)
    tabs_p = _rope_tables(pos_p, HEAD_DIM) + _rope_tables(pos_p, MLA_ROPE) + (lp // ROW_TILE,)
    tabs_s = _rope_tables(pos_s, HEAD_DIM) + _rope_tables(pos_s, MLA_ROPE) + (1,)
    place = _rope_place_matrix()
    caches = (cache_sb_k, cache_sb_v, cache_mla, cache_dsa_k, cache_dsa_v, cache_dsa_idx)
    g_final = norm_final[None, :]

    p_rows = [[] for _ in range(6)]
    s_rows = [[] for _ in range(6)]
    for l in range(depth):
        lw = _layer_weights(l, w_in, mla_q_norm, mla_kv_norm, mla_w_uq, mla_w_uk, mla_w_uv,
                            w_branch_sb, w_branch_mla, w_branch_dsa, w_out, w_up, w_down)
        g_mix, g_mlp = norm_mix[l][None, :], norm_mlp[l][None, :]
        last = l == depth - 1
        pr = _project(xp, g_mix, lw, tabs_p, ROW_TILE, place)
        y_sb, y_mla, y_dsa = _prompt_attention(pr, lw, batch, lp, n_sel_p)
        xp = _merge_mlp(xp, y_sb, y_mla, y_dsa, g_mix, g_mlp, g_final, lw, ROW_TILE, last)
        sr = _project(xs, g_mix, lw, tabs_s, db, place)
        y_sb, y_mla, y_dsa = _sample_attention(sr, lw, l, caches, page_table, n_sel_s)
        xs = _merge_mlp(xs, y_sb, y_mla, y_dsa, g_mix, g_mlp, g_final, lw, db, last)
        for rows, r, lead in ((p_rows, pr, (batch, lp)), (s_rows, sr, (db, ds))):
            width = MLA_KV_LORA + MLA_ROPE
            rows[0].append(r["k_sb"].reshape(lead + (SB_KV_HEADS, HEAD_DIM)))
            rows[1].append(r["v_sb"].reshape(lead + (SB_KV_HEADS, HEAD_DIM)))
            rows[2].append(r["row"].reshape(lead + (MLA_ROW,))[..., :width])
            rows[3].append(r["kv"].reshape(lead + (LANE,))[..., :HEAD_DIM])
            rows[4].append(r["kv"].reshape(lead + (LANE,))[..., HEAD_DIM:])
            rows[5].append(r["ikw"].reshape(lead + (LANE,))[..., :IDX_DIM])

    y_prompt = xp.reshape(batch, lp, D_MODEL)[:, N_META:t_len]
    y_sample = xs.reshape(db, ds, D_MODEL)
    outs_p = [jnp.stack([a[:, :t_len] for a in r], axis=0) for r in p_rows]
    outs_s = [jnp.stack(r, axis=0) for r in s_rows]
    return (y_prompt, y_sample, *outs_p, *outs_s)
```

```python
import functools

import jax
import jax.numpy as jnp
from jax import lax
from jax.experimental import pallas as pl
from jax.experimental.pallas import tpu as pltpu

D_MODEL = 1024
N_META = 16
HEAD_DIM = 64
SB_HEADS = 8
SB_KV_HEADS = 2
MLA_HEADS = 8
MLA_Q_LORA = 256
MLA_KV_LORA = 192
MLA_NOPE = 64
MLA_ROPE = 32
MLA_V = 64
DSA_HEADS = 8
IDX_HEADS = 16
IDX_DIM = 64
TOPK_MAX = 256
D_FF = 4 * D_MODEL
ROPE_THETA = 10000.0
EPS = 1e-6
PAGE_SIZE = 128
SB_SCALE = HEAD_DIM ** -0.5
MLA_SCALE = (MLA_NOPE + MLA_ROPE) ** -0.5
DSA_SCALE = HEAD_DIM ** -0.5
IDX_SCALE = IDX_DIM ** -0.5
IDX_W_SCALE = IDX_HEADS ** -0.5

LANE = 128
VMEM_LIMIT = 56 * 1024 * 1024

MLA_ROW = 256
ROW_TILE = 256
ATT_TQ = 256
SB_TQ = 128
SB_DEAD = -104.0
NEG_BIG = -1e30
INT_MIN = -(2 ** 31)

f32 = jnp.float32
bf16 = jnp.bfloat16
i32 = jnp.int32

_NT = (((1,), (1,)), ((), ()))


def _dot(a, b):
    return jnp.dot(a, b, preferred_element_type=f32)


def _dot_nt(a, b):
    return lax.dot_general(a, b, _NT, preferred_element_type=f32)


def _params(n_grid):
    return pltpu.CompilerParams(dimension_semantics=("arbitrary",) * n_grid,
                                vmem_limit_bytes=VMEM_LIMIT)


def _full(shape):
    nd = len(shape)
    return pl.BlockSpec(shape, lambda *_: (0,) * nd)


def _norm_rows(x, g):
    ms = jnp.mean(x * x, axis=-1, keepdims=True)
    return x * lax.rsqrt(ms + EPS) * g


def _lane_iota(shape):
    return lax.broadcasted_iota(i32, shape, len(shape) - 1)


def _rope_tile(x, cos, sin, dim):
    half = dim // 2
    first = (_lane_iota(x.shape) % dim) < half
    swapped = jnp.where(first, pltpu.roll(x, LANE - half, 1), pltpu.roll(x, half, 1))
    return x * cos + swapped * sin


def _log1p_exp_neg_abs(z):
    return jnp.log1p(jnp.exp(-jnp.abs(z)))


def _score_key(score):
    b = pltpu.bitcast(score, i32)
    return b ^ ((b >> 31) & 0x7FFFFFFF)


def _split_heads_to_lanes_lo(y, lo):
    zero = jnp.zeros_like(y)
    return jnp.where(lo, y, zero), jnp.where(lo, pltpu.roll(y, 64, 1), zero)


def _proj_sb_kernel(x_ref, g_ref, w_ref, q_ref, k_ref, v_ref):
    xn = _norm_rows(x_ref[...], g_ref[...]).astype(bf16)
    acc = _dot(xn, w_ref[...])
    tm = acc.shape[0]
    lo = _lane_iota((tm, LANE)) < 64
    for p in range(SB_HEADS // 2):
        y = acc[:, p * LANE:(p + 1) * LANE] * SB_SCALE
        yr = pltpu.roll(y, 64, 1)
        zero = jnp.zeros_like(y)
        if p < 2:
            even, odd = jnp.where(lo, y, zero), jnp.where(lo, yr, zero)
        else:
            even, odd = jnp.where(lo, zero, yr), jnp.where(lo, zero, y)
        q_ref[:, (2 * p) * LANE:(2 * p + 1) * LANE] = even.astype(bf16)
        q_ref[:, (2 * p + 1) * LANE:(2 * p + 2) * LANE] = odd.astype(bf16)
    k_ref[...] = acc[:, 512:640]
    v_ref[...] = acc[:, 640:768]


def _proj_mla_kernel(x_ref, g_ref, w_ref, qn_ref, kvn_ref, wuq_ref, wabs_ref, place_ref,
                     cos_ref, sin_ref, q_ref, row_ref, rowb_ref):
    xn = _norm_rows(x_ref[...], g_ref[...]).astype(bf16)
    acc = _dot(xn, w_ref[...])
    tm = acc.shape[0]
    cos, sin = cos_ref[...], sin_ref[...]
    cqn = _norm_rows(acc[:, :MLA_Q_LORA], qn_ref[...]).astype(bf16)
    qq = _dot(cqn, wuq_ref[...])
    q_lat = _dot(qq[:, :512].astype(bf16), wabs_ref[...])
    q_rot = jnp.concatenate(
        [_rope_tile(qq[:, 512 + t * LANE:512 + (t + 1) * LANE], cos, sin, MLA_ROPE) for t in range(2)],
        axis=1)
    placed = _dot((q_rot * MLA_SCALE).astype(bf16), place_ref[...])
    q_ref[...] = (q_lat * MLA_SCALE + placed).astype(bf16)
    g2 = acc[:, 256:512]
    lane2 = _lane_iota((tm, MLA_ROW))
    ms = jnp.sum(jnp.where(lane2 < MLA_KV_LORA, g2 * g2, 0.0), axis=-1, keepdims=True) / MLA_KV_LORA
    normed = g2 * lax.rsqrt(ms + EPS) * kvn_ref[...]
    roped = _rope_tile(g2[:, LANE:], cos, sin, MLA_ROPE)
    lane = _lane_iota((tm, LANE))
    tile2 = jnp.where(lane < 64, normed[:, LANE:], jnp.where(lane < 96, roped, 0.0))
    row = jnp.concatenate([normed[:, :LANE], tile2], axis=1)
    row_ref[...] = row
    rowb_ref[...] = row.astype(bf16)


def _proj_dsa_kernel(x_ref, g_ref, w_ref, cos_ref, sin_ref,
                     qd_ref, kv_ref, kvb_ref, qi_ref, ikw_ref, ikwb_ref):
    xn = _norm_rows(x_ref[...], g_ref[...]).astype(bf16)
    acc = _dot(xn, w_ref[...])
    tm = acc.shape[0]
    cos, sin = cos_ref[...], sin_ref[...]
    lane = _lane_iota((tm, LANE))
    lo = lane < 64
    for p in range(DSA_HEADS // 2):
        y = _rope_tile(acc[:, p * LANE:(p + 1) * LANE], cos, sin, HEAD_DIM) * DSA_SCALE
        even, odd = _split_heads_to_lanes_lo(y, lo)
        qd_ref[:, (2 * p) * LANE:(2 * p + 1) * LANE] = even.astype(bf16)
        qd_ref[:, (2 * p + 1) * LANE:(2 * p + 2) * LANE] = odd.astype(bf16)
    kv = acc[:, 512:640]
    kv = jnp.where(lo, _rope_tile(kv, cos, sin, HEAD_DIM), kv)
    kv_ref[...] = kv
    kvb_ref[...] = kv.astype(bf16)
    for p in range(IDX_HEADS // 2):
        y = _rope_tile(acc[:, 640 + p * LANE:640 + (p + 1) * LANE], cos, sin, IDX_DIM) * IDX_SCALE
        even, odd = _split_heads_to_lanes_lo(y, lo)
        qi_ref[:, (2 * p) * LANE:(2 * p + 1) * LANE] = even.astype(bf16)
        qi_ref[:, (2 * p + 1) * LANE:(2 * p + 2) * LANE] = odd.astype(bf16)
    ikw = acc[:, 1664:1792]
    ikw = jnp.where(lo, _rope_tile(ikw, cos, sin, IDX_DIM),
                    jnp.where(lane < 64 + IDX_HEADS, ikw * IDX_W_SCALE, 0.0))
    ikw_ref[...] = ikw
    ikwb_ref[...] = ikw.astype(bf16)


def _merge_kernel(x_ref, g_ref, wg_ref, ysb_ref, ymla_ref, ydsa_ref,
                  wsb_ref, wmla_ref, wdsa_ref, wout_ref, o_ref):
    x = x_ref[...]
    xn = _norm_rows(x, g_ref[...]).astype(bf16)
    gates = jax.nn.sigmoid(_dot(xn, wg_ref[...]))
    m = (gates[:, :D_MODEL] * _dot(ysb_ref[...], wsb_ref[...])
         + gates[:, D_MODEL:2 * D_MODEL] * _dot(ymla_ref[...], wmla_ref[...])
         + gates[:, 2 * D_MODEL:] * _dot(ydsa_ref[...], wdsa_ref[...]))
    o_ref[...] = x + _dot(m.astype(bf16), wout_ref[...])


def _mlp_kernel(x_ref, g_ref, wup_ref, wdown_ref, gf_ref, o_ref, *, final_norm):
    x = x_ref[...]
    xn = _norm_rows(x, g_ref[...]).astype(bf16)
    h = jnp.maximum(_dot(xn, wup_ref[...]), 0.0)
    y = x + _dot((h * h).astype(bf16), wdown_ref[...])
    if final_norm:
        y = _norm_rows(y, gf_ref[...])
    o_ref[...] = y


def _matmul_kernel(a_ref, b_ref, o_ref):
    o_ref[...] = _dot(a_ref[...], b_ref[...]).astype(o_ref.dtype)


def _sb_prompt_kernel(q_ref, k_ref, v_ref, tri_ref, o_ref, carry_sc, acc_sc, *, tq):
    qi = pl.program_id(1)
    heads = SB_HEADS
    rows = heads * tq
    q = jnp.concatenate([q_ref[:, h * LANE:(h + 1) * LANE] for h in range(heads)], axis=0)
    qpos = qi * tq + lax.broadcasted_iota(i32, (rows, LANE), 0) % tq
    lane = _lane_iota((rows, LANE))
    carry_sc[...] = jnp.zeros_like(carry_sc)
    acc_sc[...] = jnp.zeros_like(acc_sc)

    def cond(s):
        j, alive = s
        return jnp.logical_and(j >= 0, alive > 0)

    def body(s):
        j, _ = s
        off = pl.multiple_of(j * LANE, LANE)
        k = k_ref[pl.ds(off, LANE), :].astype(bf16)
        v = v_ref[pl.ds(off, LANE), :].astype(bf16)
        z = _dot_nt(q, k)
        mask = (j * LANE + lane) < qpos
        soft = _log1p_exp_neg_abs(z)
        log_beta = jnp.minimum(z, 0.0) - soft
        log_fail = jnp.where(mask, jnp.minimum(-z, 0.0) - soft, 0.0)
        hi = log_fail.astype(bf16)
        lo = (log_fail - hi.astype(f32)).astype(bf16)
        tri = tri_ref[...]
        ext = _dot(hi, tri) + _dot(lo, tri)
        carry = carry_sc[...]
        a = jnp.where(mask, jnp.exp(log_beta + ext[:, :LANE] + carry), 0.0)
        acc_sc[...] += _dot(a.astype(bf16), v)
        carry = carry + ext[:, LANE:]
        carry_sc[...] = carry
        return j - 1, (jnp.max(carry) > SB_DEAD).astype(i32)

    lax.while_loop(cond, body, (qi, jnp.int32(1)))

    acc = acc_sc[...]
    lo = _lane_iota((tq, LANE)) < 64
    for p in range(heads // 2):
        e = acc[(2 * p) * tq:(2 * p + 1) * tq]
        o = acc[(2 * p + 1) * tq:(2 * p + 2) * tq]
        if p < 2:
            y = jnp.where(lo, e, pltpu.roll(o, 64, 1))
        else:
            y = jnp.where(lo, pltpu.roll(e, 64, 1), o)
        o_ref[:, p * LANE:(p + 1) * LANE] = y.astype(o_ref.dtype)


def _tile_lanes(x, n):
    return x if n == 1 else jnp.concatenate([x] * n, axis=1)


def _fold_lanes(x):
    out = x[:, :LANE]
    for t in range(1, x.shape[1] // LANE):
        out = out + x[:, t * LANE:(t + 1) * LANE]
    return out


def _softmax_block(z, k_or_v, m_ref, l_ref, acc_ref, valid=None):
    n_t = z.shape[1] // LANE
    m_prev = m_ref[...]
    m_new = jnp.maximum(m_prev, jnp.max(z, axis=1, keepdims=True))
    alpha = jnp.exp(m_prev - m_new)
    p = jnp.exp(z - _tile_lanes(m_new, n_t))
    if valid is not None:
        p = jnp.where(valid, p, 0.0)
    l_ref[...] = alpha * l_ref[...] + _fold_lanes(p)
    acc_ref[...] = _tile_lanes(alpha, acc_ref.shape[-1] // LANE) * acc_ref[...] + _dot(p.astype(bf16), k_or_v)
    m_ref[...] = m_new


def _mla_prompt_kernel(q_ref, rows_ref, wuv_ref, o_ref, m_sc, l_sc, acc_sc, *, tq):
    qi = pl.program_id(1)
    heads = MLA_HEADS
    tk = tq
    m_sc[...] = jnp.full_like(m_sc, NEG_BIG)
    l_sc[...] = jnp.zeros_like(l_sc)
    acc_sc[...] = jnp.zeros_like(acc_sc)

    def step(j, diagonal):
        off = pl.multiple_of(j * tk, tk)
        k = rows_ref[pl.ds(off, tk), :]
        if diagonal:
            keep = _lane_iota((tq, tk)) <= lax.broadcasted_iota(i32, (tq, tk), 0)
        for h in range(heads):
            z = _dot_nt(q_ref[:, h * MLA_ROW:(h + 1) * MLA_ROW], k)
            if diagonal:
                z = jnp.where(keep, z, NEG_BIG)
            _softmax_block(z, k, m_sc.at[h], l_sc.at[h], acc_sc.at[h])

    def body(j, _):
        step(j, False)
        return 0

    lax.fori_loop(0, qi, body, 0)
    step(qi, True)
    outs = []
    for h in range(heads):
        l = jnp.sum(l_sc[h], axis=1, keepdims=True)
        outs.append((acc_sc[h] / l).astype(bf16))
    o_ref[...] = _dot(jnp.concatenate(outs, axis=1), wuv_ref[...]).astype(o_ref.dtype)


def _count_ge(key_ref, cand, n_blk, tk, ones):
    def blk(jb, cnt):
        off = pl.multiple_of(jb * tk, tk)
        kk = key_ref[:, pl.ds(off, tk)]
        for t in range(tk // LANE):
            cnt = cnt + jnp.where(kk[:, t * LANE:(t + 1) * LANE] >= cand, 1.0, 0.0)
        return cnt
    cnt = lax.fori_loop(0, n_blk, blk, jnp.zeros(cand.shape, f32))
    return _dot(cnt.astype(bf16), ones)


def _nth_largest_key(key_ref, n_blk, tk, ones, n_sel, rows):
    def cond(s):
        i, _, _, done = s
        return jnp.logical_and(i < 32, done == 0)

    def bit(s):
        i, t, cnt_t, _ = s
        cand = t + (jnp.int32(1) << (31 - i))
        tot = _count_ge(key_ref, cand, n_blk, tk, ones)
        ok = tot >= n_sel
        cnt_t = jnp.where(ok, tot, cnt_t)
        done = jnp.min(jnp.where(cnt_t == n_sel, 1.0, 0.0)).astype(i32)
        return i + 1, jnp.where(ok, cand, t), cnt_t, done

    cnt0 = jnp.zeros((rows, LANE), f32) + (n_blk * tk).astype(f32) if not isinstance(n_blk, int) \
        else jnp.full((rows, LANE), float(n_blk * tk), f32)
    _, thr, _, _ = lax.while_loop(
        cond, bit, (jnp.int32(0), jnp.full((rows, LANE), INT_MIN, i32), cnt0, jnp.int32(0)))
    return thr


def _dsa_prompt_kernel(qd_ref, qi_ref, w_ref, kv_ref, ik_ref, tri_ref, ones_ref, o_ref,
                       key_sc, wb_sc, tie_sc, m_sc, l_sc, acc_sc, *, tq, n_sel):
    qb = pl.program_id(1)
    tk = tq
    n_blk = qb + 1
    lanes_per_blk = tk // LANE
    ones = ones_ref[...]
    row_pos = qb * tq + lax.broadcasted_iota(i32, (tq, tk), 0)
    lane = _lane_iota((tq, tk))

    w = w_ref[...]
    for h in range(IDX_HEADS):
        wb_sc[h] = jnp.broadcast_to(w[:, 64 + h:65 + h], (tq, LANE))

    def score_blk(j, _):
        off = pl.multiple_of(j * tk, tk)
        ik = ik_ref[pl.ds(off, tk), :]
        score = jnp.zeros((tq, tk), f32)
        for h in range(IDX_HEADS):
            s = _dot_nt(qi_ref[:, h * LANE:(h + 1) * LANE], ik)
            score = score + jnp.maximum(s, 0.0) * _tile_lanes(wb_sc[h], lanes_per_blk)
        score = score + 0.0
        score = jnp.where((j * tk + lane) <= row_pos, score, -jnp.inf)
        key_sc[:, pl.ds(off, tk)] = _score_key(score)
        return 0

    lax.fori_loop(0, n_blk, score_blk, 0)

    thr = _nth_largest_key(key_sc, n_blk, tk, ones, n_sel, tq)
    need = n_sel - _count_ge(key_sc, thr + 1, n_blk, tk, ones)
    thr_f = jnp.concatenate([thr] * lanes_per_blk, axis=1)
    need_f = jnp.concatenate([need] * lanes_per_blk, axis=1)

    heads = DSA_HEADS
    m_sc[...] = jnp.full_like(m_sc, NEG_BIG)
    l_sc[...] = jnp.zeros_like(l_sc)
    acc_sc[...] = jnp.zeros_like(acc_sc)
    tie_sc[...] = jnp.zeros_like(tie_sc)

    def att_blk(j, _):
        off = pl.multiple_of(j * tk, tk)
        kk = key_sc[:, pl.ds(off, tk)]
        eq = kk == thr_f
        pre = _dot(jnp.where(eq, 1.0, 0.0).astype(bf16), tri_ref[...])
        tie = tie_sc[...]
        rank = pre[:, :tk] + _tile_lanes(tie, lanes_per_blk)
        take = jnp.logical_or(kk > thr_f, jnp.logical_and(eq, rank < need_f))
        valid = jnp.logical_and(take, (j * tk + lane) <= row_pos)
        tie_sc[...] = tie + pre[:, tk:]
        kvb = kv_ref[pl.ds(off, tk), :]
        for h in range(heads):
            z = _dot_nt(qd_ref[:, h * LANE:(h + 1) * LANE], kvb)
            z = jnp.where(valid, z, NEG_BIG)
            _softmax_block(z, kvb, m_sc.at[h], l_sc.at[h], acc_sc.at[h], valid=valid)
        return 0

    lax.fori_loop(0, n_blk, att_blk, 0)
    lo = _lane_iota((tq, LANE)) < 64
    outs = []
    for h in range(heads):
        outs.append(acc_sc[h] / jnp.sum(l_sc[h], axis=1, keepdims=True))
    for p in range(heads // 2):
        y = jnp.where(lo, pltpu.roll(outs[2 * p], 64, 1), outs[2 * p + 1])
        o_ref[:, p * LANE:(p + 1) * LANE] = y.astype(o_ref.dtype)


def _page_copy(cache_ref, layer, page, dst, sem):
    return pltpu.make_async_copy(cache_ref.at[layer, page], dst, sem)


def _sb_sample_kernel(pt_ref, q_ref, knew_ref, vnew_ref, ck_ref, cv_ref, tri_ref, o_ref,
                      kbuf, vbuf, sem_k, sem_v, carry_sc, acc_sc, *, layer, n_pages):
    b = pl.program_id(0)
    n_b = pl.num_programs(0)
    heads = SB_HEADS
    q = q_ref[0]
    first_slot = b & 1

    def copies(bb, j, slot):
        page = pt_ref[bb, j]
        return (_page_copy(ck_ref, layer, page, kbuf.at[slot], sem_k.at[slot]),
                _page_copy(cv_ref, layer, page, vbuf.at[slot], sem_v.at[slot]))

    def start(bb, j, slot):
        for c in copies(bb, j, slot):
            c.start()

    def wait(slot):
        for c in copies(0, 0, slot):
            c.wait()

    @pl.when(b == 0)
    def _():
        start(0, n_pages - 1, 0)

    @pl.when(b + 1 < n_b)
    def _():
        start(b + 1, n_pages - 1, 1 - first_slot)

    q_pos = n_pages * PAGE_SIZE
    new_pos = n_pages * PAGE_SIZE
    z_new = jnp.sum(q.astype(f32) * knew_ref[0].astype(bf16).astype(f32), axis=1, keepdims=True)
    new_ok = jnp.full((heads, 1), new_pos < q_pos)
    soft_new = _log1p_exp_neg_abs(z_new)
    a_new = jnp.where(new_ok, jnp.exp(jnp.minimum(z_new, 0.0) - soft_new), 0.0)
    acc_sc[...] = a_new.astype(bf16).astype(f32) * vnew_ref[0].astype(bf16).astype(f32)
    carry_sc[...] = jnp.broadcast_to(jnp.where(new_ok, jnp.minimum(-z_new, 0.0) - soft_new, 0.0),
                                     (heads, LANE))

    def cond(s):
        j, alive = s
        return jnp.logical_and(j >= 0, alive > 0)

    def body(s):
        j, _ = s
        slot = jnp.where(j == n_pages - 1, first_slot, 2 + (j & 1))
        wait(slot)

        @pl.when(j > 0)
        def _():
            start(b, j - 1, 2 + ((j - 1) & 1))

        k = kbuf[slot].astype(bf16)
        v = vbuf[slot].astype(bf16)
        z = _dot(q, k)
        soft = _log1p_exp_neg_abs(z)
        log_beta = jnp.minimum(z, 0.0) - soft
        log_fail = jnp.minimum(-z, 0.0) - soft
        hi = log_fail.astype(bf16)
        lo = (log_fail - hi.astype(f32)).astype(bf16)
        tri = tri_ref[...]
        ext = _dot(hi, tri) + _dot(lo, tri)
        carry = carry_sc[...]
        a = jnp.exp(log_beta + ext[:, :LANE] + carry)
        acc_sc[...] += _dot_nt(a.astype(bf16), v)
        carry = carry + ext[:, LANE:]
        carry_sc[...] = carry
        return j - 1, (jnp.max(carry) > SB_DEAD).astype(i32)

    j_end, _ = lax.while_loop(cond, body, (jnp.int32(n_pages - 1), jnp.int32(1)))

    @pl.when(j_end >= 0)
    def _():
        wait(2 + (j_end & 1))

    o_ref[0] = acc_sc[...]


class _PageRing:
    def __init__(self, pt_ref, cache_ref, layer, stage, sem, chunk, feat):
        self.pt_ref, self.cache_ref, self.layer = pt_ref, cache_ref, layer
        self.stage, self.sem, self.chunk, self.feat = stage, sem, chunk, feat

    def _copies(self, entry, c):
        return [pltpu.make_async_copy(
            self.cache_ref.at[self.layer, self.pt_ref[entry, c * self.chunk + i]],
            self.stage.at[c, i, pl.ds(0, self.feat), :], self.sem.at[c]) for i in range(self.chunk)]

    def start(self, entry, c):
        for cp in self._copies(entry, c):
            cp.start()

    def start_all(self, entry, n_chunks):
        for c in range(n_chunks):
            self.start(entry, c)

    def take(self, entry, c, more):
        for cp in self._copies(entry, c):
            cp.wait()
        x = self.stage[c]
        tile = jnp.concatenate([x[i] for i in range(self.chunk)], axis=1).astype(bf16)

        @pl.when(more)
        def _():
            self.start(entry + 1, c)

        return tile


def _first_lane_tile(col, rows):
    return jnp.where(_lane_iota((rows, LANE)) == 0, col, 0.0).astype(bf16)


def _mla_sample_kernel(pt_ref, q_ref, rcol_ref, cache_ref, o_ref, stage, kb, z_sc, sem,
                       *, layer, n_pages, chunk):
    b = pl.program_id(0)
    more = b + 1 < pl.num_programs(0)
    n_chunks = n_pages // chunk
    span = chunk * PAGE_SIZE
    p_len = n_pages * PAGE_SIZE
    width = MLA_KV_LORA + MLA_ROPE
    ring = _PageRing(pt_ref, cache_ref, layer, stage, sem, chunk, width)

    @pl.when(b == 0)
    def _():
        stage[:, :, width:, :] = jnp.zeros((n_chunks, chunk, MLA_ROW - width, LANE), f32)
        ring.start_all(0, n_chunks)

    q = q_ref[0]

    def scores(c, _):
        kc = ring.take(b, c, more)
        off = pl.multiple_of(c * span, span)
        kb[:, pl.ds(off, span)] = kc
        z_sc[:, pl.ds(off, span)] = _dot(q, kc)
        return 0

    lax.fori_loop(0, n_chunks, scores, 0)
    knew = _first_lane_tile(rcol_ref[0], MLA_ROW)
    kb[:, pl.ds(p_len, LANE)] = knew
    z_sc[:, pl.ds(p_len, LANE)] = jnp.where(_lane_iota((MLA_HEADS, LANE)) == 0, _dot(q, knew), NEG_BIG)
    z = z_sc[...]
    p = jnp.exp(z - jnp.max(z, axis=1, keepdims=True))
    p = (p / jnp.sum(p, axis=1, keepdims=True)).astype(bf16)
    o_ref[0] = _dot_nt(kb[...], p)


def _idx_sample_kernel(pt_ref, qi_ref, w_ref, ikcol_ref, cache_ref, o_ref, stage, sem,
                       *, layer, n_pages, chunk):
    b = pl.program_id(0)
    more = b + 1 < pl.num_programs(0)
    n_chunks = n_pages // chunk
    span = chunk * PAGE_SIZE
    ring = _PageRing(pt_ref, cache_ref, layer, stage, sem, chunk, IDX_DIM)

    @pl.when(b == 0)
    def _():
        ring.start_all(0, n_chunks)

    qk = qi_ref[0][:, :IDX_DIM]
    wb = jnp.broadcast_to(w_ref[0], (IDX_HEADS, LANE))

    def score_of(s):
        return jnp.sum(jnp.maximum(s, 0.0) * _tile_lanes(wb, s.shape[1] // LANE), axis=0, keepdims=True) + 0.0

    def body(c, _):
        kc = ring.take(b, c, more)
        o_ref[0, :, pl.ds(pl.multiple_of(c * span, span), span)] = score_of(_dot(qk, kc))
        return 0

    lax.fori_loop(0, n_chunks, body, 0)
    s_new = score_of(_dot(qk, _first_lane_tile(ikcol_ref[0], IDX_DIM)))
    o_ref[0, :, pl.ds(n_pages * PAGE_SIZE, LANE)] = jnp.where(_lane_iota((1, LANE)) == 0, s_new, -jnp.inf)


def _select_sample_kernel(score_ref, tri_ref, ones_ref, sel_ref, key_sc, *, n_sel, n_tiles):
    rows = score_ref.shape[0]
    ones = ones_ref[...]

    def to_key(t, _):
        off = pl.multiple_of(t * LANE, LANE)
        key_sc[:, pl.ds(off, LANE)] = _score_key(score_ref[:, pl.ds(off, LANE)])
        return 0

    lax.fori_loop(0, n_tiles, to_key, 0)
    thr = _nth_largest_key(key_sc, n_tiles, LANE, ones, n_sel, rows)
    need = n_sel - _count_ge(key_sc, thr + 1, n_tiles, LANE, ones)

    def sel_tile(t, tie):
        off = pl.multiple_of(t * LANE, LANE)
        kk = key_sc[:, pl.ds(off, LANE)]
        eq = kk == thr
        pre = _dot(jnp.where(eq, 1.0, 0.0).astype(bf16), tri_ref[...])
        take = jnp.logical_or(kk > thr, jnp.logical_and(eq, (pre[:, :LANE] + tie) < need))
        sel_ref[:, pl.ds(off, LANE)] = jnp.where(take, 1.0, 0.0)
        return tie + pre[:, LANE:]

    lax.fori_loop(0, n_tiles, sel_tile, jnp.zeros((rows, LANE), f32))


def _dsa_sample_kernel(pt_ref, q_ref, sel_ref, kcol_ref, vcol_ref, ck_ref, cv_ref, o_ref,
                       kstage, vstage, z_sc, p_sc, sem_k, sem_v, *, layer, n_pages, chunk):
    b = pl.program_id(0)
    more = b + 1 < pl.num_programs(0)
    n_chunks = n_pages // chunk
    span = chunk * PAGE_SIZE
    p_len = n_pages * PAGE_SIZE
    kring = _PageRing(pt_ref, ck_ref, layer, kstage, sem_k, chunk, HEAD_DIM)
    vring = _PageRing(pt_ref, cv_ref, layer, vstage, sem_v, chunk, HEAD_DIM)

    @pl.when(b == 0)
    def _():
        kring.start_all(0, n_chunks)
        vring.start_all(0, n_chunks)

    qk = q_ref[0][:, :HEAD_DIM]

    def masked_scores(kc, off, n):
        valid = sel_ref[0, :, pl.ds(off, n)] > 0.5
        z_sc[:, pl.ds(off, n)] = jnp.where(valid, _dot(qk, kc), NEG_BIG)

    def scores(c, _):
        masked_scores(kring.take(b, c, more), pl.multiple_of(c * span, span), span)
        return 0

    lax.fori_loop(0, n_chunks, scores, 0)
    masked_scores(_first_lane_tile(kcol_ref[0], HEAD_DIM), p_len, LANE)
    z = z_sc[...]
    p = jnp.where(z > 0.5 * NEG_BIG, jnp.exp(z - jnp.max(z, axis=1, keepdims=True)), 0.0)
    p_sc[...] = (p / jnp.sum(p, axis=1, keepdims=True)).astype(bf16)

    def values(c, acc):
        vc = vring.take(b, c, more)
        return acc + _dot_nt(vc, p_sc[:, pl.ds(pl.multiple_of(c * span, span), span)])

    acc = lax.fori_loop(0, n_chunks, values, jnp.zeros((HEAD_DIM, DSA_HEADS), f32))
    o_ref[0] = acc + _dot_nt(_first_lane_tile(vcol_ref[0], HEAD_DIM), p_sc[:, pl.ds(p_len, LANE)])


def _rope_tables(pos, dim):
    half = dim // 2
    inv_freq = ROPE_THETA ** (-jnp.arange(half, dtype=f32) / half)
    ang = pos.astype(f32)[:, None] * inv_freq[None, :]
    cos, sin = jnp.cos(ang), jnp.sin(ang)
    reps = LANE // dim
    return (jnp.tile(jnp.concatenate([cos, cos], axis=-1), (1, reps)),
            jnp.tile(jnp.concatenate([-sin, sin], axis=-1), (1, reps)))


def _layer_weights(l, w_in, mla_q_norm, mla_kv_norm, mla_w_uq, mla_w_uk, mla_w_uv,
                   w_branch_sb, w_branch_mla, w_branch_dsa, w_out, w_up, w_down):
    w = w_in[l].astype(bf16)
    c = 0
    cols = {}
    for name, n in (("sb", 768), ("mla_cq", 256), ("mla_kv", 224), ("dsa_q", 512), ("dsa_kv", 128),
                    ("idx_q", 1024), ("idx_kw", 80), ("gates", 3 * D_MODEL)):
        cols[name] = w[:, c:c + n]
        c += n
    pad = lambda a, n: jnp.pad(a, ((0, 0), (0, n - a.shape[1])))
    w_mla = jnp.concatenate([cols["mla_cq"], pad(cols["mla_kv"], MLA_ROW)], axis=1)
    w_dsa = jnp.concatenate([cols["dsa_q"], cols["dsa_kv"], cols["idx_q"], pad(cols["idx_kw"], LANE)], axis=1)
    uq = mla_w_uq[l].astype(bf16)
    wuq = jnp.concatenate([uq[:, :, :MLA_NOPE].reshape(MLA_Q_LORA, MLA_HEADS * MLA_NOPE),
                           uq[:, :, MLA_NOPE:].reshape(MLA_Q_LORA, MLA_HEADS * MLA_ROPE)], axis=1)
    uk = mla_w_uk[l].astype(bf16)
    uv = mla_w_uv[l].astype(bf16)
    wabs = jnp.zeros((MLA_HEADS * MLA_NOPE, MLA_HEADS * MLA_ROW), bf16)
    wuv = jnp.zeros((MLA_HEADS * MLA_ROW, MLA_HEADS * MLA_V), bf16)
    for h in range(MLA_HEADS):
        wabs = wabs.at[h * MLA_NOPE:(h + 1) * MLA_NOPE, h * MLA_ROW:h * MLA_ROW + MLA_KV_LORA].set(uk[:, h, :].T)
        wuv = wuv.at[h * MLA_ROW:h * MLA_ROW + MLA_KV_LORA, h * MLA_V:(h + 1) * MLA_V].set(uv[:, h, :])
    return dict(
        w_sb=cols["sb"], w_mla=w_mla, w_dsa=w_dsa, w_gates=cols["gates"],
        qn=mla_q_norm[l][None, :], kvn=jnp.pad(mla_kv_norm[l], (0, MLA_ROW - MLA_KV_LORA))[None, :],
        wuq=wuq, wabs=wabs, wuv=wuv,
        w_bsb=w_branch_sb[l].astype(bf16), w_bmla=w_branch_mla[l].astype(bf16),
        w_bdsa=w_branch_dsa[l].astype(bf16), w_out=w_out[l].astype(bf16),
        w_up=w_up[l].astype(bf16), w_down=w_down[l].astype(bf16))


def _rope_place_matrix():
    src = jnp.arange(MLA_HEADS * MLA_ROPE)
    dst = (src // MLA_ROPE) * MLA_ROW + MLA_KV_LORA + src % MLA_ROPE
    return jnp.zeros((MLA_HEADS * MLA_ROPE, MLA_HEADS * MLA_ROW), bf16).at[src, dst].set(1)


def _suffix_matrix(n):
    j = jnp.arange(n)[:, None]
    s = jnp.arange(n)[None, :]
    return jnp.concatenate([(j > s), jnp.ones((n, LANE), bool)], axis=1).astype(bf16)


def _prefix_matrix(n):
    j = jnp.arange(n)[:, None]
    s = jnp.arange(n)[None, :]
    return jnp.concatenate([(j < s), jnp.ones((n, LANE), bool)], axis=1).astype(bf16)


def _row_call(kernel, n_rows, tm, row_ins, const_ins, pos_ins, outs):
    in_specs = [pl.BlockSpec((tm, a.shape[1]), lambda i: (i, 0)) for a in row_ins]
    in_specs += [_full(a.shape) for a in const_ins]
    for a, nb in pos_ins:
        in_specs.append(pl.BlockSpec((tm, a.shape[1]), functools.partial(lambda i, nb: (i % nb, 0), nb=nb)))
    return pl.pallas_call(
        kernel,
        grid=(n_rows // tm,),
        in_specs=in_specs,
        out_specs=[pl.BlockSpec((tm, c), lambda i: (i, 0)) for c, _ in outs],
        out_shape=[jax.ShapeDtypeStruct((n_rows, c), d) for c, d in outs],
        compiler_params=_params(1),
    )(*row_ins, *const_ins, *[a for a, _ in pos_ins])


def _project(x, g, lw, tabs, tm, place):
    n = x.shape[0]
    cos64, sin64, cos32, sin32, nb = tabs
    q_sb, k_sb, v_sb = _row_call(
        _proj_sb_kernel, n, tm, [x], [g, lw["w_sb"]], [],
        [(SB_HEADS * LANE, bf16), (LANE, f32), (LANE, f32)])
    q_mla, row, rowb = _row_call(
        _proj_mla_kernel, n, tm, [x],
        [g, lw["w_mla"], lw["qn"], lw["kvn"], lw["wuq"], lw["wabs"], place],
        [(cos32, nb), (sin32, nb)],
        [(MLA_HEADS * MLA_ROW, bf16), (MLA_ROW, f32), (MLA_ROW, bf16)])
    q_dsa, kv, kvb, q_idx, ikw, ikwb = _row_call(
        _proj_dsa_kernel, n, tm, [x], [g, lw["w_dsa"]], [(cos64, nb), (sin64, nb)],
        [(DSA_HEADS * LANE, bf16), (LANE, f32), (LANE, bf16),
         (IDX_HEADS * LANE, bf16), (LANE, f32), (LANE, bf16)])
    return dict(q_sb=q_sb, k_sb=k_sb, v_sb=v_sb, q_mla=q_mla, row=row, rowb=rowb,
                q_dsa=q_dsa, kv=kv, kvb=kvb, q_idx=q_idx, ikw=ikw, ikwb=ikwb)


def _merge_mlp(x, y_sb, y_mla, y_dsa, g_mix, g_mlp, g_final, lw, tm, final_norm):
    n = x.shape[0]
    x1 = pl.pallas_call(
        _merge_kernel,
        grid=(n // tm,),
        in_specs=[pl.BlockSpec((tm, D_MODEL), lambda i: (i, 0)), _full(g_mix.shape), _full(lw["w_gates"].shape),
                  pl.BlockSpec((tm, 512), lambda i: (i, 0)), pl.BlockSpec((tm, 512), lambda i: (i, 0)),
                  pl.BlockSpec((tm, 512), lambda i: (i, 0)),
                  _full(lw["w_bsb"].shape), _full(lw["w_bmla"].shape), _full(lw["w_bdsa"].shape),
                  _full(lw["w_out"].shape)],
        out_specs=pl.BlockSpec((tm, D_MODEL), lambda i: (i, 0)),
        out_shape=jax.ShapeDtypeStruct((n, D_MODEL), f32),
        compiler_params=_params(1),
    )(x, g_mix, lw["w_gates"], y_sb, y_mla, y_dsa, lw["w_bsb"], lw["w_bmla"], lw["w_bdsa"], lw["w_out"])
    return pl.pallas_call(
        functools.partial(_mlp_kernel, final_norm=final_norm),
        grid=(n // tm,),
        in_specs=[pl.BlockSpec((tm, D_MODEL), lambda i: (i, 0)), _full(g_mlp.shape),
                  _full(lw["w_up"].shape), _full(lw["w_down"].shape), _full(g_final.shape)],
        out_specs=pl.BlockSpec((tm, D_MODEL), lambda i: (i, 0)),
        out_shape=jax.ShapeDtypeStruct((n, D_MODEL), f32),
        compiler_params=_params(1),
    )(x1, g_mlp, lw["w_up"], lw["w_down"], g_final)


def _prompt_attention(pr, lw, batch, lp, n_sel):
    n = batch * lp
    nq = lp // SB_TQ
    y_sb = pl.pallas_call(
        functools.partial(_sb_prompt_kernel, tq=SB_TQ),
        grid=(batch, nq),
        in_specs=[pl.BlockSpec((SB_TQ, SB_HEADS * LANE), lambda b, i: (b * nq + i, 0)),
                  pl.BlockSpec((lp, LANE), lambda b, i: (b, 0)),
                  pl.BlockSpec((lp, LANE), lambda b, i: (b, 0)),
                  _full((LANE, 2 * LANE))],
        out_specs=pl.BlockSpec((SB_TQ, 512), lambda b, i: (b * nq + i, 0)),
        out_shape=jax.ShapeDtypeStruct((n, 512), bf16),
        scratch_shapes=[pltpu.VMEM((SB_HEADS * SB_TQ, LANE), f32), pltpu.VMEM((SB_HEADS * SB_TQ, LANE), f32)],
        compiler_params=_params(2),
    )(pr["q_sb"], pr["k_sb"], pr["v_sb"], _suffix_matrix(LANE))
    tq = ATT_TQ
    nq = lp // tq
    y_mla = pl.pallas_call(
        functools.partial(_mla_prompt_kernel, tq=tq),
        grid=(batch, nq),
        in_specs=[pl.BlockSpec((tq, MLA_HEADS * MLA_ROW), lambda b, i: (b * nq + i, 0)),
                  pl.BlockSpec((lp, MLA_ROW), lambda b, i: (b, 0)),
                  _full(lw["wuv"].shape)],
        out_specs=pl.BlockSpec((tq, 512), lambda b, i: (b * nq + i, 0)),
        out_shape=jax.ShapeDtypeStruct((n, 512), bf16),
        scratch_shapes=[pltpu.VMEM((MLA_HEADS, tq, LANE), f32), pltpu.VMEM((MLA_HEADS, tq, LANE), f32),
                        pltpu.VMEM((MLA_HEADS, tq, MLA_ROW), f32)],
        compiler_params=_params(2),
    )(pr["q_mla"], pr["rowb"], lw["wuv"])
    y_dsa = pl.pallas_call(
        functools.partial(_dsa_prompt_kernel, tq=tq, n_sel=n_sel),
        grid=(batch, nq),
        in_specs=[pl.BlockSpec((tq, DSA_HEADS * LANE), lambda b, i: (b * nq + i, 0)),
                  pl.BlockSpec((tq, IDX_HEADS * LANE), lambda b, i: (b * nq + i, 0)),
                  pl.BlockSpec((tq, LANE), lambda b, i: (b * nq + i, 0)),
                  pl.BlockSpec((lp, LANE), lambda b, i: (b, 0)),
                  pl.BlockSpec((lp, LANE), lambda b, i: (b, 0)),
                  _full((tq, tq + LANE)), _full((LANE, LANE))],
        out_specs=pl.BlockSpec((tq, 512), lambda b, i: (b * nq + i, 0)),
        out_shape=jax.ShapeDtypeStruct((n, 512), bf16),
        scratch_shapes=[pltpu.VMEM((tq, lp), i32), pltpu.VMEM((IDX_HEADS, tq, LANE), f32),
                        pltpu.VMEM((tq, LANE), f32),
                        pltpu.VMEM((DSA_HEADS, tq, LANE), f32), pltpu.VMEM((DSA_HEADS, tq, LANE), f32),
                        pltpu.VMEM((DSA_HEADS, tq, LANE), f32)],
        compiler_params=_params(2),
    )(pr["q_dsa"], pr["q_idx"], pr["ikw"], pr["kvb"], pr["ikwb"], _prefix_matrix(tq),
      jnp.ones((LANE, LANE), bf16))
    return y_sb, y_mla, y_dsa


def _pages_keys_on_lanes(cache):
    depth, pool, page = cache.shape[:3]
    nd = cache.ndim
    return cache.transpose((0, 1) + tuple(range(3, nd)) + (2,)).reshape(depth, pool, -1, page)


def _sample_attention(pr, lw, layer, caches, page_table, n_sel):
    ck, cv, cache_mla, cache_dsa_k, cache_dsa_v, cache_dsa_idx = caches
    db, n_pages = page_table.shape
    p_len = n_pages * PAGE_SIZE
    chunk = 8 if n_pages % 8 == 0 else n_pages
    n_chunks = n_pages // chunk
    any_spec = pl.BlockSpec(memory_space=pl.ANY)
    per_b3 = lambda shape: pl.BlockSpec((1,) + shape, lambda b, pt: (b, 0, 0))
    col = lambda a: a.reshape(a.shape + (1,))

    o_sb = pl.pallas_call(
        functools.partial(_sb_sample_kernel, layer=layer, n_pages=n_pages),
        grid_spec=pltpu.PrefetchScalarGridSpec(
            num_scalar_prefetch=1, grid=(db,),
            in_specs=[per_b3((SB_HEADS, LANE)), per_b3((1, LANE)), per_b3((1, LANE)), any_spec, any_spec,
                      pl.BlockSpec((LANE, 2 * LANE), lambda b, pt: (0, 0))],
            out_specs=per_b3((SB_HEADS, LANE)),
            scratch_shapes=[pltpu.VMEM((4, LANE, PAGE_SIZE), f32), pltpu.VMEM((4, LANE, PAGE_SIZE), f32),
                            pltpu.SemaphoreType.DMA((4,)), pltpu.SemaphoreType.DMA((4,)),
                            pltpu.VMEM((SB_HEADS, LANE), f32), pltpu.VMEM((SB_HEADS, LANE), f32)]),
        out_shape=jax.ShapeDtypeStruct((db, SB_HEADS, LANE), f32),
        compiler_params=_params(1),
    )(page_table, pr["q_sb"].reshape(db, SB_HEADS, LANE), pr["k_sb"].reshape(db, 1, LANE),
      pr["v_sb"].reshape(db, 1, LANE), ck, cv, _suffix_matrix(LANE))
    half = jnp.arange(SB_HEADS) // (SB_HEADS // SB_KV_HEADS)
    y_sb = jnp.where(half[None, :, None] == 0, o_sb[:, :, :HEAD_DIM], o_sb[:, :, HEAD_DIM:])
    y_sb = y_sb.reshape(db, SB_HEADS * HEAD_DIM).astype(bf16)

    kp = p_len + LANE
    o_mla = pl.pallas_call(
        functools.partial(_mla_sample_kernel, layer=layer, n_pages=n_pages, chunk=chunk),
        grid_spec=pltpu.PrefetchScalarGridSpec(
            num_scalar_prefetch=1, grid=(db,),
            in_specs=[per_b3((MLA_HEADS, MLA_ROW)), per_b3((MLA_ROW, 1)), any_spec],
            out_specs=per_b3((MLA_ROW, MLA_HEADS)),
            scratch_shapes=[pltpu.VMEM((n_chunks, chunk, MLA_ROW, PAGE_SIZE), f32),
                            pltpu.VMEM((MLA_ROW, kp), bf16), pltpu.VMEM((MLA_HEADS, kp), f32),
                            pltpu.SemaphoreType.DMA((n_chunks,))]),
        out_shape=jax.ShapeDtypeStruct((db, MLA_ROW, MLA_HEADS), f32),
        compiler_params=_params(1),
    )(page_table, pr["q_mla"].reshape(db, MLA_HEADS, MLA_ROW), col(pr["row"]), cache_mla)
    o_mla = o_mla.transpose(0, 2, 1).reshape(db, MLA_HEADS * MLA_ROW).astype(bf16)
    y_mla = pl.pallas_call(
        _matmul_kernel,
        in_specs=[_full((db, MLA_HEADS * MLA_ROW)), _full(lw["wuv"].shape)],
        out_specs=_full((db, 512)),
        out_shape=jax.ShapeDtypeStruct((db, 512), bf16),
        grid=(1,),
        compiler_params=_params(1),
    )(o_mla, lw["wuv"])

    scores = pl.pallas_call(
        functools.partial(_idx_sample_kernel, layer=layer, n_pages=n_pages, chunk=chunk),
        grid_spec=pltpu.PrefetchScalarGridSpec(
            num_scalar_prefetch=1, grid=(db,),
            in_specs=[per_b3((IDX_HEADS, LANE)), per_b3((IDX_HEADS, 1)), per_b3((IDX_DIM, 1)), any_spec],
            out_specs=per_b3((1, kp)),
            scratch_shapes=[pltpu.VMEM((n_chunks, chunk, IDX_DIM, PAGE_SIZE), f32),
                            pltpu.SemaphoreType.DMA((n_chunks,))]),
        out_shape=jax.ShapeDtypeStruct((db, 1, kp), f32),
        compiler_params=_params(1),
    )(page_table, pr["q_idx"].reshape(db, IDX_HEADS, LANE), col(pr["ikw"][:, 64:64 + IDX_HEADS]),
      col(pr["ikw"][:, :IDX_DIM]), cache_dsa_idx)
    rb = 32 if db % 32 == 0 else db
    sel = pl.pallas_call(
        functools.partial(_select_sample_kernel, n_sel=n_sel, n_tiles=kp // LANE),
        grid=(db // rb,),
        in_specs=[pl.BlockSpec((rb, kp), lambda i: (i, 0)), _full((LANE, 2 * LANE)), _full((LANE, LANE))],
        out_specs=pl.BlockSpec((rb, kp), lambda i: (i, 0)),
        out_shape=jax.ShapeDtypeStruct((db, kp), f32),
        scratch_shapes=[pltpu.VMEM((rb, kp), i32)],
        compiler_params=_params(1),
    )(scores.reshape(db, kp), _prefix_matrix(LANE), jnp.ones((LANE, LANE), bf16))
    o_dsa = pl.pallas_call(
        functools.partial(_dsa_sample_kernel, layer=layer, n_pages=n_pages, chunk=chunk),
        grid_spec=pltpu.PrefetchScalarGridSpec(
            num_scalar_prefetch=1, grid=(db,),
            in_specs=[per_b3((DSA_HEADS, LANE)), per_b3((1, kp)), per_b3((HEAD_DIM, 1)), per_b3((HEAD_DIM, 1)),
                      any_spec, any_spec],
            out_specs=per_b3((HEAD_DIM, DSA_HEADS)),
            scratch_shapes=[pltpu.VMEM((n_chunks, chunk, HEAD_DIM, PAGE_SIZE), f32),
                            pltpu.VMEM((n_chunks, chunk, HEAD_DIM, PAGE_SIZE), f32),
                            pltpu.VMEM((DSA_HEADS, kp), f32), pltpu.VMEM((DSA_HEADS, kp), bf16),
                            pltpu.SemaphoreType.DMA((n_chunks,)), pltpu.SemaphoreType.DMA((n_chunks,))]),
        out_shape=jax.ShapeDtypeStruct((db, HEAD_DIM, DSA_HEADS), f32),
        compiler_params=_params(1),
    )(page_table, pr["q_dsa"].reshape(db, DSA_HEADS, LANE), sel.reshape(db, 1, kp),
      col(pr["kv"][:, :HEAD_DIM]), col(pr["kv"][:, HEAD_DIM:]), cache_dsa_k, cache_dsa_v)
    y_dsa = o_dsa.transpose(0, 2, 1).reshape(db, DSA_HEADS * HEAD_DIM).astype(bf16)
    return y_sb, y_mla, y_dsa


def kernel(x_prompt, x_sample, cache_sb_k, cache_sb_v, cache_mla, cache_dsa_k, cache_dsa_v, cache_dsa_idx, page_table, meta, norm_mix, w_in, mla_q_norm, mla_kv_norm, mla_w_uq, mla_w_uk, mla_w_uv, w_branch_sb, w_branch_mla, w_branch_dsa, w_out, norm_mlp, w_up, w_down, norm_final):
    batch, seq, _ = x_prompt.shape
    db, ds, _ = x_sample.shape
    assert ds == 1, "one new token per decode entry"
    depth = w_in.shape[0]
    n_pages = page_table.shape[1]
    p_len = n_pages * PAGE_SIZE
    t_len = seq + N_META
    lp = -(-t_len // ATT_TQ) * ATT_TQ
    n_sel_p = min(TOPK_MAX, seq // 4)
    n_sel_s = min(TOPK_MAX, (p_len + ds) // 4)

    xp = jnp.concatenate([jnp.broadcast_to(meta.astype(f32)[None], (batch, N_META, D_MODEL)), x_prompt], axis=1)
    xp = jnp.pad(xp, ((0, 0), (0, lp - t_len), (0, 0))).reshape(batch * lp, D_MODEL)
    xs = x_sample.reshape(db * ds, D_MODEL)

    pos_p = jnp.arange(lp)
    pos_s = jnp.full((db,), p_len, jnp.int32)
    tabs_p = _rope_tables(pos_p, HEAD_DIM) + _rope_tables(pos_p, MLA_ROPE) + (lp // ROW_TILE,)
    tabs_s = _rope_tables(pos_s, HEAD_DIM) + _rope_tables(pos_s, MLA_ROPE) + (1,)
    place = _rope_place_matrix()
    caches = tuple(_pages_keys_on_lanes(c) for c in
                   (cache_sb_k, cache_sb_v, cache_mla, cache_dsa_k, cache_dsa_v, cache_dsa_idx))
    g_final = norm_final[None, :]

    p_rows = [[] for _ in range(6)]
    s_rows = [[] for _ in range(6)]
    for l in range(depth):
        lw = _layer_weights(l, w_in, mla_q_norm, mla_kv_norm, mla_w_uq, mla_w_uk, mla_w_uv,
                            w_branch_sb, w_branch_mla, w_branch_dsa, w_out, w_up, w_down)
        g_mix, g_mlp = norm_mix[l][None, :], norm_mlp[l][None, :]
        last = l == depth - 1
        pr = _project(xp, g_mix, lw, tabs_p, ROW_TILE, place)
        y_sb, y_mla, y_dsa = _prompt_attention(pr, lw, batch, lp, n_sel_p)
        xp = _merge_mlp(xp, y_sb, y_mla, y_dsa, g_mix, g_mlp, g_final, lw, ROW_TILE, last)
        sr = _project(xs, g_mix, lw, tabs_s, db, place)
        y_sb, y_mla, y_dsa = _sample_attention(sr, lw, l, caches, page_table, n_sel_s)
        xs = _merge_mlp(xs, y_sb, y_mla, y_dsa, g_mix, g_mlp, g_final, lw, db, last)
        for rows, r, lead in ((p_rows, pr, (batch, lp)), (s_rows, sr, (db, ds))):
            width = MLA_KV_LORA + MLA_ROPE
            rows[0].append(r["k_sb"].reshape(lead + (SB_KV_HEADS, HEAD_DIM)))
            rows[1].append(r["v_sb"].reshape(lead + (SB_KV_HEADS, HEAD_DIM)))
            rows[2].append(r["row"].reshape(lead + (MLA_ROW,))[..., :width])
            rows[3].append(r["kv"].reshape(lead + (LANE,))[..., :HEAD_DIM])
            rows[4].append(r["kv"].reshape(lead + (LANE,))[..., HEAD_DIM:])
            rows[5].append(r["ikw"].reshape(lead + (LANE,))[..., :IDX_DIM])

    y_prompt = xp.reshape(batch, lp, D_MODEL)[:, N_META:t_len]
    y_sample = xs.reshape(db, ds, D_MODEL)
    outs_p = [jnp.stack([a[:, :t_len] for a in r], axis=0) for r in p_rows]
    outs_s = [jnp.stack(r, axis=0) for r in s_rows]
    return (y_prompt, y_sample, *outs_p, *outs_s)
```

```python
import functools

import jax
import jax.numpy as jnp
from jax import lax
from jax.experimental import pallas as pl
from jax.experimental.pallas import tpu as pltpu

D_MODEL = 1024
N_META = 16
HEAD_DIM = 64
SB_HEADS = 8
SB_KV_HEADS = 2
MLA_HEADS = 8
MLA_Q_LORA = 256
MLA_KV_LORA = 192
MLA_NOPE = 64
MLA_ROPE = 32
MLA_V = 64
DSA_HEADS = 8
IDX_HEADS = 16
IDX_DIM = 64
TOPK_MAX = 256
D_FF = 4 * D_MODEL
ROPE_THETA = 10000.0
EPS = 1e-6
PAGE_SIZE = 128
SB_SCALE = HEAD_DIM ** -0.5
MLA_SCALE = (MLA_NOPE + MLA_ROPE) ** -0.5
DSA_SCALE = HEAD_DIM ** -0.5
IDX_SCALE = IDX_DIM ** -0.5
IDX_W_SCALE = IDX_HEADS ** -0.5

LANE = 128
VMEM_LIMIT = 56 * 1024 * 1024

MLA_ROW = 256
ROW_TILE = 256
ATT_TQ = 256
SB_TQ = 128
SB_DEAD = -104.0
NEG_BIG = -1e30
F32_MIN = -3.4028234663852886e38
INT_MIN = -(2 ** 31)

f32 = jnp.float32
bf16 = jnp.bfloat16
i32 = jnp.int32

_NT = (((1,), (1,)), ((), ()))


def _dot(a, b):
    return jnp.dot(a, b, preferred_element_type=f32)


def _dot_nt(a, b):
    return lax.dot_general(a, b, _NT, preferred_element_type=f32)


def _params(n_grid):
    return pltpu.CompilerParams(dimension_semantics=("arbitrary",) * n_grid,
                                vmem_limit_bytes=VMEM_LIMIT)


def _full(shape):
    nd = len(shape)
    return pl.BlockSpec(shape, lambda *_: (0,) * nd)


def _norm_rows(x, g):
    ms = jnp.mean(x * x, axis=-1, keepdims=True)
    return x * lax.rsqrt(ms + EPS) * g


def _lane_iota(shape):
    return lax.broadcasted_iota(i32, shape, len(shape) - 1)


def _rope_tile(x, cos, sin, dim):
    half = dim // 2
    first = (_lane_iota(x.shape) % dim) < half
    swapped = jnp.where(first, pltpu.roll(x, LANE - half, 1), pltpu.roll(x, half, 1))
    return x * cos + swapped * sin


def _log1p_exp_neg_abs(z):
    return jnp.log1p(jnp.exp(-jnp.abs(z)))


def _score_key(score):
    b = pltpu.bitcast(score, i32)
    return b ^ ((b >> 31) & 0x7FFFFFFF)


def _split_heads_to_lanes_lo(y, lo):
    zero = jnp.zeros_like(y)
    return jnp.where(lo, y, zero), jnp.where(lo, pltpu.roll(y, 64, 1), zero)


def _proj_sb_kernel(x_ref, g_ref, w_ref, q_ref, k_ref, v_ref):
    xn = _norm_rows(x_ref[...], g_ref[...]).astype(bf16)
    acc = _dot(xn, w_ref[...])
    tm = acc.shape[0]
    lo = _lane_iota((tm, LANE)) < 64
    for p in range(SB_HEADS // 2):
        y = acc[:, p * LANE:(p + 1) * LANE] * SB_SCALE
        yr = pltpu.roll(y, 64, 1)
        zero = jnp.zeros_like(y)
        if p < 2:
            even, odd = jnp.where(lo, y, zero), jnp.where(lo, yr, zero)
        else:
            even, odd = jnp.where(lo, zero, yr), jnp.where(lo, zero, y)
        q_ref[:, (2 * p) * LANE:(2 * p + 1) * LANE] = even.astype(bf16)
        q_ref[:, (2 * p + 1) * LANE:(2 * p + 2) * LANE] = odd.astype(bf16)
    k_ref[...] = acc[:, 512:640]
    v_ref[...] = acc[:, 640:768]


def _proj_mla_kernel(x_ref, g_ref, w_ref, qn_ref, kvn_ref, wuq_ref, wabs_ref, place_ref,
                     cos_ref, sin_ref, q_ref, row_ref, rowb_ref):
    xn = _norm_rows(x_ref[...], g_ref[...]).astype(bf16)
    acc = _dot(xn, w_ref[...])
    tm = acc.shape[0]
    cos, sin = cos_ref[...], sin_ref[...]
    cqn = _norm_rows(acc[:, :MLA_Q_LORA], qn_ref[...]).astype(bf16)
    qq = _dot(cqn, wuq_ref[...])
    q_lat = _dot(qq[:, :512].astype(bf16), wabs_ref[...])
    q_rot = jnp.concatenate(
        [_rope_tile(qq[:, 512 + t * LANE:512 + (t + 1) * LANE], cos, sin, MLA_ROPE) for t in range(2)],
        axis=1)
    placed = _dot((q_rot * MLA_SCALE).astype(bf16), place_ref[...])
    q_ref[...] = (q_lat * MLA_SCALE + placed).astype(bf16)
    g2 = acc[:, 256:512]
    lane2 = _lane_iota((tm, MLA_ROW))
    ms = jnp.sum(jnp.where(lane2 < MLA_KV_LORA, g2 * g2, 0.0), axis=-1, keepdims=True) / MLA_KV_LORA
    normed = g2 * lax.rsqrt(ms + EPS) * kvn_ref[...]
    roped = _rope_tile(g2[:, LANE:], cos, sin, MLA_ROPE)
    lane = _lane_iota((tm, LANE))
    tile2 = jnp.where(lane < 64, normed[:, LANE:], jnp.where(lane < 96, roped, 0.0))
    row = jnp.concatenate([normed[:, :LANE], tile2], axis=1)
    row_ref[...] = row
    rowb_ref[...] = row.astype(bf16)


def _proj_dsa_kernel(x_ref, g_ref, w_ref, cos_ref, sin_ref,
                     qd_ref, kv_ref, kvb_ref, qi_ref, ikw_ref, ikwb_ref):
    xn = _norm_rows(x_ref[...], g_ref[...]).astype(bf16)
    acc = _dot(xn, w_ref[...])
    tm = acc.shape[0]
    cos, sin = cos_ref[...], sin_ref[...]
    lane = _lane_iota((tm, LANE))
    lo = lane < 64
    for p in range(DSA_HEADS // 2):
        y = _rope_tile(acc[:, p * LANE:(p + 1) * LANE], cos, sin, HEAD_DIM) * DSA_SCALE
        even, odd = _split_heads_to_lanes_lo(y, lo)
        qd_ref[:, (2 * p) * LANE:(2 * p + 1) * LANE] = even.astype(bf16)
        qd_ref[:, (2 * p + 1) * LANE:(2 * p + 2) * LANE] = odd.astype(bf16)
    kv = acc[:, 512:640]
    kv = jnp.where(lo, _rope_tile(kv, cos, sin, HEAD_DIM), kv)
    kv_ref[...] = kv
    kvb_ref[...] = kv.astype(bf16)
    for p in range(IDX_HEADS // 2):
        y = _rope_tile(acc[:, 640 + p * LANE:640 + (p + 1) * LANE], cos, sin, IDX_DIM) * IDX_SCALE
        even, odd = _split_heads_to_lanes_lo(y, lo)
        qi_ref[:, (2 * p) * LANE:(2 * p + 1) * LANE] = even.astype(bf16)
        qi_ref[:, (2 * p + 1) * LANE:(2 * p + 2) * LANE] = odd.astype(bf16)
    ikw = acc[:, 1664:1792]
    ikw = jnp.where(lo, _rope_tile(ikw, cos, sin, IDX_DIM),
                    jnp.where(lane < 64 + IDX_HEADS, ikw * IDX_W_SCALE, 0.0))
    ikw_ref[...] = ikw
    ikwb_ref[...] = ikw.astype(bf16)


def _merge_kernel(x_ref, g_ref, wg_ref, ysb_ref, ymla_ref, ydsa_ref,
                  wsb_ref, wmla_ref, wdsa_ref, wout_ref, o_ref):
    x = x_ref[...]
    xn = _norm_rows(x, g_ref[...]).astype(bf16)
    gates = jax.nn.sigmoid(_dot(xn, wg_ref[...]))
    m = (gates[:, :D_MODEL] * _dot(ysb_ref[...], wsb_ref[...])
         + gates[:, D_MODEL:2 * D_MODEL] * _dot(ymla_ref[...], wmla_ref[...])
         + gates[:, 2 * D_MODEL:] * _dot(ydsa_ref[...], wdsa_ref[...]))
    o_ref[...] = x + _dot(m.astype(bf16), wout_ref[...])


def _mlp_kernel(x_ref, g_ref, wup_ref, wdown_ref, gf_ref, o_ref, *, final_norm):
    x = x_ref[...]
    xn = _norm_rows(x, g_ref[...]).astype(bf16)
    h = jnp.maximum(_dot(xn, wup_ref[...]), 0.0)
    y = x + _dot((h * h).astype(bf16), wdown_ref[...])
    if final_norm:
        y = _norm_rows(y, gf_ref[...])
    o_ref[...] = y


def _matmul_kernel(a_ref, b_ref, o_ref):
    o_ref[...] = _dot(a_ref[...], b_ref[...]).astype(o_ref.dtype)


def _sb_prompt_kernel(q_ref, k_ref, v_ref, tri_ref, o_ref, carry_sc, acc_sc, *, tq):
    qi = pl.program_id(1)
    heads = SB_HEADS
    rows = heads * tq
    q = jnp.concatenate([q_ref[:, h * LANE:(h + 1) * LANE] for h in range(heads)], axis=0)
    qpos = qi * tq + lax.broadcasted_iota(i32, (rows, LANE), 0) % tq
    lane = _lane_iota((rows, LANE))
    carry_sc[...] = jnp.zeros_like(carry_sc)
    acc_sc[...] = jnp.zeros_like(acc_sc)

    def cond(s):
        j, alive = s
        return jnp.logical_and(j >= 0, alive > 0)

    def body(s):
        j, _ = s
        off = pl.multiple_of(j * LANE, LANE)
        k = k_ref[pl.ds(off, LANE), :].astype(bf16)
        v = v_ref[pl.ds(off, LANE), :].astype(bf16)
        z = _dot_nt(q, k)
        mask = (j * LANE + lane) < qpos
        soft = _log1p_exp_neg_abs(z)
        log_beta = jnp.minimum(z, 0.0) - soft
        log_fail = jnp.where(mask, jnp.minimum(-z, 0.0) - soft, 0.0)
        hi = log_fail.astype(bf16)
        lo = (log_fail - hi.astype(f32)).astype(bf16)
        tri = tri_ref[...]
        ext = _dot(hi, tri) + _dot(lo, tri)
        carry = carry_sc[...]
        a = jnp.where(mask, jnp.exp(log_beta + ext[:, :LANE] + carry), 0.0)
        acc_sc[...] += _dot(a.astype(bf16), v)
        carry = carry + ext[:, LANE:]
        carry_sc[...] = carry
        return j - 1, (jnp.max(carry) > SB_DEAD).astype(i32)

    lax.while_loop(cond, body, (qi, jnp.int32(1)))

    acc = acc_sc[...]
    lo = _lane_iota((tq, LANE)) < 64
    for p in range(heads // 2):
        e = acc[(2 * p) * tq:(2 * p + 1) * tq]
        o = acc[(2 * p + 1) * tq:(2 * p + 2) * tq]
        if p < 2:
            y = jnp.where(lo, e, pltpu.roll(o, 64, 1))
        else:
            y = jnp.where(lo, pltpu.roll(e, 64, 1), o)
        o_ref[:, p * LANE:(p + 1) * LANE] = y.astype(o_ref.dtype)


def _tile_lanes(x, n):
    return x if n == 1 else jnp.concatenate([x] * n, axis=1)


def _fold_lanes(x):
    out = x[:, :LANE]
    for t in range(1, x.shape[1] // LANE):
        out = out + x[:, t * LANE:(t + 1) * LANE]
    return out


def _softmax_block(z, k_or_v, m_ref, l_ref, acc_ref):
    n_t = z.shape[1] // LANE
    m_prev = m_ref[...]
    m_new = jnp.maximum(m_prev, jnp.max(z, axis=1, keepdims=True))
    alpha = jnp.exp(m_prev - m_new)
    p = jnp.exp(z - _tile_lanes(m_new, n_t))
    l_ref[...] = alpha * l_ref[...] + _fold_lanes(p)
    acc_ref[...] = _tile_lanes(alpha, acc_ref.shape[-1] // LANE) * acc_ref[...] + _dot(p.astype(bf16), k_or_v)
    m_ref[...] = m_new


def _mla_prompt_kernel(q_ref, rows_ref, wuv_ref, o_ref, m_sc, l_sc, acc_sc, *, tq):
    qi = pl.program_id(1)
    heads = MLA_HEADS
    tk = tq
    m_sc[...] = jnp.full_like(m_sc, NEG_BIG)
    l_sc[...] = jnp.zeros_like(l_sc)
    acc_sc[...] = jnp.zeros_like(acc_sc)

    def step(j, diagonal):
        off = pl.multiple_of(j * tk, tk)
        k = rows_ref[pl.ds(off, tk), :]
        if diagonal:
            keep = _lane_iota((tq, tk)) <= lax.broadcasted_iota(i32, (tq, tk), 0)
        for h in range(heads):
            z = _dot_nt(q_ref[:, h * MLA_ROW:(h + 1) * MLA_ROW], k)
            if diagonal:
                z = jnp.where(keep, z, NEG_BIG)
            _softmax_block(z, k, m_sc.at[h], l_sc.at[h], acc_sc.at[h])

    def body(j, _):
        step(j, False)
        return 0

    lax.fori_loop(0, qi, body, 0)
    step(qi, True)
    outs = []
    for h in range(heads):
        l = jnp.sum(l_sc[h], axis=1, keepdims=True)
        outs.append((acc_sc[h] / l).astype(bf16))
    o_ref[...] = _dot(jnp.concatenate(outs, axis=1), wuv_ref[...]).astype(o_ref.dtype)


def _count_ge(key_ref, cand, n_blk, tk, ones):
    def blk(jb, cnt):
        off = pl.multiple_of(jb * tk, tk)
        kk = key_ref[:, pl.ds(off, tk)]
        for t in range(tk // LANE):
            cnt = cnt + jnp.where(kk[:, t * LANE:(t + 1) * LANE] >= cand, 1.0, 0.0)
        return cnt
    cnt = lax.fori_loop(0, n_blk, blk, jnp.zeros(cand.shape, f32))
    return _dot(cnt.astype(bf16), ones)


SEARCH_ROWS = 128
SEL_BLOCK = 1024


def _top_n_threshold(key_ref, n_blk, tk, ones, n_sel, rows):
    grp = SEARCH_ROWS if rows % SEARCH_ROWS == 0 else rows
    thrs, needs = [], []
    for r0 in range(0, rows, grp):
        ref = key_ref.at[pl.ds(r0, grp), :]

        def bit(i, t):
            cand = t + (jnp.int32(1) << (31 - i))
            return jnp.where(_count_ge(ref, cand, n_blk, tk, ones) >= n_sel, cand, t)

        thr = lax.fori_loop(0, 32, bit, jnp.full((grp, LANE), INT_MIN, i32))
        thrs.append(thr)
        needs.append(n_sel - _count_ge(ref, thr + 1, n_blk, tk, ones))
    return jnp.concatenate(thrs, axis=0), jnp.concatenate(needs, axis=0)


def _dsa_prompt_kernel(qd_ref, qi_ref, w_ref, kv_ref, ik_ref, tri_ref, ones_ref, o_ref,
                       key_sc, wb_sc, tie_sc, m_sc, l_sc, acc_sc, *, tq, n_sel):
    qb = pl.program_id(1)
    tk = tq
    n_blk = qb + 1
    lanes_per_blk = tk // LANE
    ones = ones_ref[...]
    row_pos = qb * tq + lax.broadcasted_iota(i32, (tq, tk), 0)
    lane = _lane_iota((tq, tk))

    w = w_ref[...]
    for h in range(IDX_HEADS):
        wb_sc[h] = jnp.broadcast_to(w[:, 64 + h:65 + h], (tq, LANE))

    def score_blk(j, _):
        off = pl.multiple_of(j * tk, tk)
        ik = ik_ref[pl.ds(off, tk), :]
        score = jnp.zeros((tq, tk), f32)
        for h in range(IDX_HEADS):
            s = _dot_nt(qi_ref[:, h * LANE:(h + 1) * LANE], ik)
            score = score + jnp.maximum(s, 0.0) * _tile_lanes(wb_sc[h], lanes_per_blk)
        score = score + 0.0
        score = jnp.where((j * tk + lane) <= row_pos, score, -jnp.inf)
        key_sc[:, pl.ds(off, tk)] = _score_key(score)
        return 0

    lax.fori_loop(0, n_blk, score_blk, 0)

    thr, need = _top_n_threshold(key_sc, n_blk, tk, ones, n_sel, tq)
    thr_f = _tile_lanes(thr, lanes_per_blk)
    need_f = _tile_lanes(need, lanes_per_blk)

    heads = DSA_HEADS
    m_sc[...] = jnp.full_like(m_sc, F32_MIN)
    l_sc[...] = jnp.zeros_like(l_sc)
    acc_sc[...] = jnp.zeros_like(acc_sc)
    tie_sc[...] = jnp.zeros_like(tie_sc)

    def att_blk(j, _):
        off = pl.multiple_of(j * tk, tk)
        kk = key_sc[:, pl.ds(off, tk)]
        eq = kk == thr_f
        pre = _dot(jnp.where(eq, 1.0, 0.0).astype(bf16), tri_ref[...])
        tie = tie_sc[...]
        rank = pre[:, :tk] + _tile_lanes(tie, lanes_per_blk)
        take = jnp.logical_or(kk > thr_f, jnp.logical_and(eq, rank < need_f))
        valid = jnp.logical_and(take, (j * tk + lane) <= row_pos)
        tie_sc[...] = tie + pre[:, tk:]
        kvb = kv_ref[pl.ds(off, tk), :]
        for h in range(heads):
            z = _dot_nt(qd_ref[:, h * LANE:(h + 1) * LANE], kvb)
            _softmax_block(jnp.where(valid, z, -jnp.inf), kvb, m_sc.at[h], l_sc.at[h], acc_sc.at[h])
        return 0

    lax.fori_loop(0, n_blk, att_blk, 0)
    lo = _lane_iota((tq, LANE)) < 64
    outs = []
    for h in range(heads):
        outs.append(acc_sc[h] / jnp.sum(l_sc[h], axis=1, keepdims=True))
    for p in range(heads // 2):
        y = jnp.where(lo, pltpu.roll(outs[2 * p], 64, 1), outs[2 * p + 1])
        o_ref[:, p * LANE:(p + 1) * LANE] = y.astype(o_ref.dtype)


def _page_copy(cache_ref, layer, page, dst, sem):
    return pltpu.make_async_copy(cache_ref.at[layer, page], dst, sem)


def _sb_sample_kernel(pt_ref, q_ref, knew_ref, vnew_ref, ck_ref, cv_ref, tri_ref, o_ref,
                      kbuf, vbuf, sem_k, sem_v, carry_sc, acc_sc, *, layer, n_pages):
    b = pl.program_id(0)
    n_b = pl.num_programs(0)
    heads = SB_HEADS
    q = q_ref[0]
    first_slot = b & 1

    def copies(bb, j, slot):
        page = pt_ref[bb, j]
        return (_page_copy(ck_ref, layer, page, kbuf.at[slot], sem_k.at[slot]),
                _page_copy(cv_ref, layer, page, vbuf.at[slot], sem_v.at[slot]))

    def start(bb, j, slot):
        for c in copies(bb, j, slot):
            c.start()

    def wait(slot):
        for c in copies(0, 0, slot):
            c.wait()

    @pl.when(b == 0)
    def _():
        start(0, n_pages - 1, 0)

    @pl.when(b + 1 < n_b)
    def _():
        start(b + 1, n_pages - 1, 1 - first_slot)

    q_pos = n_pages * PAGE_SIZE
    new_pos = n_pages * PAGE_SIZE
    z_new = jnp.sum(q.astype(f32) * knew_ref[0].astype(bf16).astype(f32), axis=1, keepdims=True)
    new_ok = jnp.full((heads, 1), new_pos < q_pos)
    soft_new = _log1p_exp_neg_abs(z_new)
    a_new = jnp.where(new_ok, jnp.exp(jnp.minimum(z_new, 0.0) - soft_new), 0.0)
    acc_sc[...] = a_new.astype(bf16).astype(f32) * vnew_ref[0].astype(bf16).astype(f32)
    carry_sc[...] = jnp.broadcast_to(jnp.where(new_ok, jnp.minimum(-z_new, 0.0) - soft_new, 0.0),
                                     (heads, LANE))

    def cond(s):
        j, alive = s
        return jnp.logical_and(j >= 0, alive > 0)

    def body(s):
        j, _ = s
        slot = jnp.where(j == n_pages - 1, first_slot, 2 + (j & 1))
        wait(slot)

        @pl.when(j > 0)
        def _():
            start(b, j - 1, 2 + ((j - 1) & 1))

        k = kbuf[slot].astype(bf16)
        v = vbuf[slot].astype(bf16)
        z = _dot(q, k)
        soft = _log1p_exp_neg_abs(z)
        log_beta = jnp.minimum(z, 0.0) - soft
        log_fail = jnp.minimum(-z, 0.0) - soft
        hi = log_fail.astype(bf16)
        lo = (log_fail - hi.astype(f32)).astype(bf16)
        tri = tri_ref[...]
        ext = _dot(hi, tri) + _dot(lo, tri)
        carry = carry_sc[...]
        a = jnp.exp(log_beta + ext[:, :LANE] + carry)
        acc_sc[...] += _dot_nt(a.astype(bf16), v)
        carry = carry + ext[:, LANE:]
        carry_sc[...] = carry
        return j - 1, (jnp.max(carry) > SB_DEAD).astype(i32)

    j_end, _ = lax.while_loop(cond, body, (jnp.int32(n_pages - 1), jnp.int32(1)))

    @pl.when(j_end >= 0)
    def _():
        wait(2 + (j_end & 1))

    o_ref[0] = acc_sc[...]


class _PageRing:
    def __init__(self, pt_ref, cache_ref, layer, stage, sem, chunk, feat):
        self.pt_ref, self.cache_ref, self.layer = pt_ref, cache_ref, layer
        self.stage, self.sem, self.chunk, self.feat = stage, sem, chunk, feat

    def _copies(self, entry, c):
        return [pltpu.make_async_copy(
            self.cache_ref.at[self.layer, self.pt_ref[entry, c * self.chunk + i]],
            self.stage.at[c, i, pl.ds(0, self.feat), :], self.sem.at[c]) for i in range(self.chunk)]

    def start(self, entry, c):
        for cp in self._copies(entry, c):
            cp.start()

    def start_all(self, entry, n_chunks):
        for c in range(n_chunks):
            self.start(entry, c)

    def wait(self, c):
        pltpu.make_async_copy(self.cache_ref.at[self.layer, pl.ds(0, self.chunk)],
                              self.stage.at[c, :, pl.ds(0, self.feat), :], self.sem.at[c]).wait()

    def tile(self, c):
        x = self.stage[c]
        return jnp.concatenate([x[i] for i in range(self.chunk)], axis=1).astype(bf16)

    def consume(self, entry, more, n_chunks, fn, group=4):
        for g0 in range(0, n_chunks, group):
            cs = range(g0, min(g0 + group, n_chunks))
            for c in cs:
                self.wait(c)
            for c in cs:
                fn(c, self.tile(c))

            @pl.when(more)
            def _():
                for c in cs:
                    self.start(entry + 1, c)


def _first_lane_tile(col, rows):
    return jnp.where(_lane_iota((rows, LANE)) == 0, col, 0.0).astype(bf16)


def _mla_sample_kernel(pt_ref, q_ref, rcol_ref, cache_ref, o_ref, stage, kb, z_sc, sem,
                       *, layer, n_pages, chunk):
    b = pl.program_id(0)
    more = b + 1 < pl.num_programs(0)
    n_chunks = n_pages // chunk
    span = chunk * PAGE_SIZE
    p_len = n_pages * PAGE_SIZE
    width = MLA_KV_LORA + MLA_ROPE
    ring = _PageRing(pt_ref, cache_ref, layer, stage, sem, chunk, width)

    @pl.when(b == 0)
    def _():
        stage[:, :, width:, :] = jnp.zeros((n_chunks, chunk, MLA_ROW - width, LANE), f32)
        ring.start_all(0, n_chunks)

    q = q_ref[0]

    def scores(c, kc):
        kb[:, c * span:(c + 1) * span] = kc
        z_sc[:, c * span:(c + 1) * span] = _dot(q, kc)

    ring.consume(b, more, n_chunks, scores)
    knew = _first_lane_tile(rcol_ref[0], MLA_ROW)
    kb[:, pl.ds(p_len, LANE)] = knew
    z_sc[:, pl.ds(p_len, LANE)] = jnp.where(_lane_iota((MLA_HEADS, LANE)) == 0, _dot(q, knew), NEG_BIG)
    z = z_sc[...]
    p = jnp.exp(z - jnp.max(z, axis=1, keepdims=True))
    p = (p / jnp.sum(p, axis=1, keepdims=True)).astype(bf16)
    o_ref[0] = _dot_nt(kb[...], p)


def _idx_sample_kernel(pt_ref, qi_ref, w_ref, ikcol_ref, cache_ref, o_ref, stage, sem,
                       *, layer, n_pages, chunk):
    b = pl.program_id(0)
    more = b + 1 < pl.num_programs(0)
    n_chunks = n_pages // chunk
    span = chunk * PAGE_SIZE
    ring = _PageRing(pt_ref, cache_ref, layer, stage, sem, chunk, IDX_DIM)

    @pl.when(b == 0)
    def _():
        ring.start_all(0, n_chunks)

    qk = qi_ref[0][:, :IDX_DIM]
    wb = jnp.broadcast_to(w_ref[0], (IDX_HEADS, LANE))

    def score_of(s):
        return jnp.sum(jnp.maximum(s, 0.0) * _tile_lanes(wb, s.shape[1] // LANE), axis=0, keepdims=True) + 0.0

    def body(c, kc):
        o_ref[0, :, c * span:(c + 1) * span] = score_of(_dot(qk, kc))

    ring.consume(b, more, n_chunks, body)
    p_len = n_pages * PAGE_SIZE
    s_new = score_of(_dot(qk, _first_lane_tile(ikcol_ref[0], IDX_DIM)))
    o_ref[0, :, p_len:p_len + LANE] = jnp.where(_lane_iota((1, LANE)) == 0, s_new, -jnp.inf)
    pad = o_ref.shape[2] - (p_len + LANE)
    if pad:
        o_ref[0, :, p_len + LANE:] = jnp.full((1, pad), -jnp.inf, f32)


def _select_sample_kernel(score_ref, tri_ref, ones_ref, sel_ref, key_sc, *, n_sel, n_blk, n_keys):
    rows = score_ref.shape[0]
    ones = ones_ref[...]
    tk = SEL_BLOCK

    def to_key(j, _):
        off = pl.multiple_of(j * tk, tk)
        key_sc[:, pl.ds(off, tk)] = _score_key(score_ref[:, pl.ds(off, tk)])
        return 0

    lax.fori_loop(0, n_blk, to_key, 0)
    thr, need = _top_n_threshold(key_sc, n_blk, tk, ones, n_sel, rows)
    lane = _lane_iota((rows, LANE))

    def sel_blk(j, tie):
        off = pl.multiple_of(j * tk, tk)
        kk = key_sc[:, pl.ds(off, tk)]
        sel = []
        for t in range(tk // LANE):
            kt = kk[:, t * LANE:(t + 1) * LANE]
            eq = kt == thr
            pre = _dot(jnp.where(eq, 1.0, 0.0).astype(bf16), tri_ref[...])
            take = jnp.logical_or(kt > thr, jnp.logical_and(eq, (pre[:, :LANE] + tie) < need))
            take = jnp.logical_and(take, j * tk + t * LANE + lane < n_keys)
            sel.append(jnp.where(take, 1.0, 0.0))
            tie = tie + pre[:, LANE:]
        sel_ref[:, pl.ds(off, tk)] = jnp.concatenate(sel, axis=1)
        return tie

    lax.fori_loop(0, n_blk, sel_blk, jnp.zeros((rows, LANE), f32))


def _dsa_sample_kernel(pt_ref, q_ref, sel_ref, kcol_ref, vcol_ref, ck_ref, cv_ref, o_ref,
                       kstage, vstage, z_sc, p_sc, sem_k, sem_v, *, layer, n_pages, chunk):
    b = pl.program_id(0)
    more = b + 1 < pl.num_programs(0)
    n_chunks = n_pages // chunk
    span = chunk * PAGE_SIZE
    p_len = n_pages * PAGE_SIZE
    kring = _PageRing(pt_ref, ck_ref, layer, kstage, sem_k, chunk, HEAD_DIM)
    vring = _PageRing(pt_ref, cv_ref, layer, vstage, sem_v, chunk, HEAD_DIM)

    @pl.when(b == 0)
    def _():
        kring.start_all(0, n_chunks)
        vring.start_all(0, n_chunks)

    qk = q_ref[0][:, :HEAD_DIM]

    def masked_scores(kc, off, n):
        valid = sel_ref[0, :, off:off + n] > 0.5
        z_sc[:, off:off + n] = jnp.where(valid, _dot(qk, kc), NEG_BIG)

    kring.consume(b, more, n_chunks, lambda c, kc: masked_scores(kc, c * span, span))
    masked_scores(_first_lane_tile(kcol_ref[0], HEAD_DIM), p_len, LANE)
    z = z_sc[...]
    p = jnp.where(z > 0.5 * NEG_BIG, jnp.exp(z - jnp.max(z, axis=1, keepdims=True)), 0.0)
    p_sc[...] = (p / jnp.sum(p, axis=1, keepdims=True)).astype(bf16)

    parts = [_dot_nt(_first_lane_tile(vcol_ref[0], HEAD_DIM), p_sc[:, p_len:p_len + LANE])]
    vring.consume(b, more, n_chunks,
                  lambda c, vc: parts.append(_dot_nt(vc, p_sc[:, c * span:(c + 1) * span])))
    o_ref[0] = sum(parts[1:], parts[0])


def _rope_tables(pos, dim):
    half = dim // 2
    inv_freq = ROPE_THETA ** (-jnp.arange(half, dtype=f32) / half)
    ang = pos.astype(f32)[:, None] * inv_freq[None, :]
    cos, sin = jnp.cos(ang), jnp.sin(ang)
    reps = LANE // dim
    return (jnp.tile(jnp.concatenate([cos, cos], axis=-1), (1, reps)),
            jnp.tile(jnp.concatenate([-sin, sin], axis=-1), (1, reps)))


def _layer_weights(l, w_in, mla_q_norm, mla_kv_norm, mla_w_uq, mla_w_uk, mla_w_uv,
                   w_branch_sb, w_branch_mla, w_branch_dsa, w_out, w_up, w_down):
    w = w_in[l].astype(bf16)
    c = 0
    cols = {}
    for name, n in (("sb", 768), ("mla_cq", 256), ("mla_kv", 224), ("dsa_q", 512), ("dsa_kv", 128),
                    ("idx_q", 1024), ("idx_kw", 80), ("gates", 3 * D_MODEL)):
        cols[name] = w[:, c:c + n]
        c += n
    pad = lambda a, n: jnp.pad(a, ((0, 0), (0, n - a.shape[1])))
    w_mla = jnp.concatenate([cols["mla_cq"], pad(cols["mla_kv"], MLA_ROW)], axis=1)
    w_dsa = jnp.concatenate([cols["dsa_q"], cols["dsa_kv"], cols["idx_q"], pad(cols["idx_kw"], LANE)], axis=1)
    uq = mla_w_uq[l].astype(bf16)
    wuq = jnp.concatenate([uq[:, :, :MLA_NOPE].reshape(MLA_Q_LORA, MLA_HEADS * MLA_NOPE),
                           uq[:, :, MLA_NOPE:].reshape(MLA_Q_LORA, MLA_HEADS * MLA_ROPE)], axis=1)
    uk = mla_w_uk[l].astype(bf16)
    uv = mla_w_uv[l].astype(bf16)
    wabs = jnp.zeros((MLA_HEADS * MLA_NOPE, MLA_HEADS * MLA_ROW), bf16)
    wuv = jnp.zeros((MLA_HEADS * MLA_ROW, MLA_HEADS * MLA_V), bf16)
    for h in range(MLA_HEADS):
        wabs = wabs.at[h * MLA_NOPE:(h + 1) * MLA_NOPE, h * MLA_ROW:h * MLA_ROW + MLA_KV_LORA].set(uk[:, h, :].T)
        wuv = wuv.at[h * MLA_ROW:h * MLA_ROW + MLA_KV_LORA, h * MLA_V:(h + 1) * MLA_V].set(uv[:, h, :])
    return dict(
        w_sb=cols["sb"], w_mla=w_mla, w_dsa=w_dsa, w_gates=cols["gates"],
        qn=mla_q_norm[l][None, :], kvn=jnp.pad(mla_kv_norm[l], (0, MLA_ROW - MLA_KV_LORA))[None, :],
        wuq=wuq, wabs=wabs, wuv=wuv,
        w_bsb=w_branch_sb[l].astype(bf16), w_bmla=w_branch_mla[l].astype(bf16),
        w_bdsa=w_branch_dsa[l].astype(bf16), w_out=w_out[l].astype(bf16),
        w_up=w_up[l].astype(bf16), w_down=w_down[l].astype(bf16))


def _rope_place_matrix():
    src = jnp.arange(MLA_HEADS * MLA_ROPE)
    dst = (src // MLA_ROPE) * MLA_ROW + MLA_KV_LORA + src % MLA_ROPE
    return jnp.zeros((MLA_HEADS * MLA_ROPE, MLA_HEADS * MLA_ROW), bf16).at[src, dst].set(1)


def _suffix_matrix(n):
    j = jnp.arange(n)[:, None]
    s = jnp.arange(n)[None, :]
    return jnp.concatenate([(j > s), jnp.ones((n, LANE), bool)], axis=1).astype(bf16)


def _prefix_matrix(n):
    j = jnp.arange(n)[:, None]
    s = jnp.arange(n)[None, :]
    return jnp.concatenate([(j < s), jnp.ones((n, LANE), bool)], axis=1).astype(bf16)


def _row_call(kernel, n_rows, tm, row_ins, const_ins, pos_ins, outs):
    in_specs = [pl.BlockSpec((tm, a.shape[1]), lambda i: (i, 0)) for a in row_ins]
    in_specs += [_full(a.shape) for a in const_ins]
    for a, nb in pos_ins:
        in_specs.append(pl.BlockSpec((tm, a.shape[1]), functools.partial(lambda i, nb: (i % nb, 0), nb=nb)))
    return pl.pallas_call(
        kernel,
        grid=(n_rows // tm,),
        in_specs=in_specs,
        out_specs=[pl.BlockSpec((tm, c), lambda i: (i, 0)) for c, _ in outs],
        out_shape=[jax.ShapeDtypeStruct((n_rows, c), d) for c, d in outs],
        compiler_params=_params(1),
    )(*row_ins, *const_ins, *[a for a, _ in pos_ins])


def _project(x, g, lw, tabs, tm, place):
    n = x.shape[0]
    cos64, sin64, cos32, sin32, nb = tabs
    q_sb, k_sb, v_sb = _row_call(
        _proj_sb_kernel, n, tm, [x], [g, lw["w_sb"]], [],
        [(SB_HEADS * LANE, bf16), (LANE, f32), (LANE, f32)])
    q_mla, row, rowb = _row_call(
        _proj_mla_kernel, n, tm, [x],
        [g, lw["w_mla"], lw["qn"], lw["kvn"], lw["wuq"], lw["wabs"], place],
        [(cos32, nb), (sin32, nb)],
        [(MLA_HEADS * MLA_ROW, bf16), (MLA_ROW, f32), (MLA_ROW, bf16)])
    q_dsa, kv, kvb, q_idx, ikw, ikwb = _row_call(
        _proj_dsa_kernel, n, tm, [x], [g, lw["w_dsa"]], [(cos64, nb), (sin64, nb)],
        [(DSA_HEADS * LANE, bf16), (LANE, f32), (LANE, bf16),
         (IDX_HEADS * LANE, bf16), (LANE, f32), (LANE, bf16)])
    return dict(q_sb=q_sb, k_sb=k_sb, v_sb=v_sb, q_mla=q_mla, row=row, rowb=rowb,
                q_dsa=q_dsa, kv=kv, kvb=kvb, q_idx=q_idx, ikw=ikw, ikwb=ikwb)


def _merge_mlp(x, y_sb, y_mla, y_dsa, g_mix, g_mlp, g_final, lw, tm, final_norm):
    n = x.shape[0]
    x1 = pl.pallas_call(
        _merge_kernel,
        grid=(n // tm,),
        in_specs=[pl.BlockSpec((tm, D_MODEL), lambda i: (i, 0)), _full(g_mix.shape), _full(lw["w_gates"].shape),
                  pl.BlockSpec((tm, 512), lambda i: (i, 0)), pl.BlockSpec((tm, 512), lambda i: (i, 0)),
                  pl.BlockSpec((tm, 512), lambda i: (i, 0)),
                  _full(lw["w_bsb"].shape), _full(lw["w_bmla"].shape), _full(lw["w_bdsa"].shape),
                  _full(lw["w_out"].shape)],
        out_specs=pl.BlockSpec((tm, D_MODEL), lambda i: (i, 0)),
        out_shape=jax.ShapeDtypeStruct((n, D_MODEL), f32),
        compiler_params=_params(1),
    )(x, g_mix, lw["w_gates"], y_sb, y_mla, y_dsa, lw["w_bsb"], lw["w_bmla"], lw["w_bdsa"], lw["w_out"])
    return pl.pallas_call(
        functools.partial(_mlp_kernel, final_norm=final_norm),
        grid=(n // tm,),
        in_specs=[pl.BlockSpec((tm, D_MODEL), lambda i: (i, 0)), _full(g_mlp.shape),
                  _full(lw["w_up"].shape), _full(lw["w_down"].shape), _full(g_final.shape)],
        out_specs=pl.BlockSpec((tm, D_MODEL), lambda i: (i, 0)),
        out_shape=jax.ShapeDtypeStruct((n, D_MODEL), f32),
        compiler_params=_params(1),
    )(x1, g_mlp, lw["w_up"], lw["w_down"], g_final)


def _prompt_attention(pr, lw, batch, lp, n_sel):
    n = batch * lp
    nq = lp // SB_TQ
    y_sb = pl.pallas_call(
        functools.partial(_sb_prompt_kernel, tq=SB_TQ),
        grid=(batch, nq),
        in_specs=[pl.BlockSpec((SB_TQ, SB_HEADS * LANE), lambda b, i: (b * nq + i, 0)),
                  pl.BlockSpec((lp, LANE), lambda b, i: (b, 0)),
                  pl.BlockSpec((lp, LANE), lambda b, i: (b, 0)),
                  _full((LANE, 2 * LANE))],
        out_specs=pl.BlockSpec((SB_TQ, 512), lambda b, i: (b * nq + i, 0)),
        out_shape=jax.ShapeDtypeStruct((n, 512), bf16),
        scratch_shapes=[pltpu.VMEM((SB_HEADS * SB_TQ, LANE), f32), pltpu.VMEM((SB_HEADS * SB_TQ, LANE), f32)],
        compiler_params=_params(2),
    )(pr["q_sb"], pr["k_sb"], pr["v_sb"], _suffix_matrix(LANE))
    tq = ATT_TQ
    nq = lp // tq
    y_mla = pl.pallas_call(
        functools.partial(_mla_prompt_kernel, tq=tq),
        grid=(batch, nq),
        in_specs=[pl.BlockSpec((tq, MLA_HEADS * MLA_ROW), lambda b, i: (b * nq + i, 0)),
                  pl.BlockSpec((lp, MLA_ROW), lambda b, i: (b, 0)),
                  _full(lw["wuv"].shape)],
        out_specs=pl.BlockSpec((tq, 512), lambda b, i: (b * nq + i, 0)),
        out_shape=jax.ShapeDtypeStruct((n, 512), bf16),
        scratch_shapes=[pltpu.VMEM((MLA_HEADS, tq, LANE), f32), pltpu.VMEM((MLA_HEADS, tq, LANE), f32),
                        pltpu.VMEM((MLA_HEADS, tq, MLA_ROW), f32)],
        compiler_params=_params(2),
    )(pr["q_mla"], pr["rowb"], lw["wuv"])
    y_dsa = pl.pallas_call(
        functools.partial(_dsa_prompt_kernel, tq=tq, n_sel=n_sel),
        grid=(batch, nq),
        in_specs=[pl.BlockSpec((tq, DSA_HEADS * LANE), lambda b, i: (b * nq + i, 0)),
                  pl.BlockSpec((tq, IDX_HEADS * LANE), lambda b, i: (b * nq + i, 0)),
                  pl.BlockSpec((tq, LANE), lambda b, i: (b * nq + i, 0)),
                  pl.BlockSpec((lp, LANE), lambda b, i: (b, 0)),
                  pl.BlockSpec((lp, LANE), lambda b, i: (b, 0)),
                  _full((tq, tq + LANE)), _full((LANE, LANE))],
        out_specs=pl.BlockSpec((tq, 512), lambda b, i: (b * nq + i, 0)),
        out_shape=jax.ShapeDtypeStruct((n, 512), bf16),
        scratch_shapes=[pltpu.VMEM((tq, lp), i32), pltpu.VMEM((IDX_HEADS, tq, LANE), f32),
                        pltpu.VMEM((tq, LANE), f32),
                        pltpu.VMEM((DSA_HEADS, tq, LANE), f32), pltpu.VMEM((DSA_HEADS, tq, LANE), f32),
                        pltpu.VMEM((DSA_HEADS, tq, LANE), f32)],
        compiler_params=_params(2),
    )(pr["q_dsa"], pr["q_idx"], pr["ikw"], pr["kvb"], pr["ikwb"], _prefix_matrix(tq),
      jnp.ones((LANE, LANE), bf16))
    return y_sb, y_mla, y_dsa


def _pages_keys_on_lanes(cache):
    depth, pool, page = cache.shape[:3]
    nd = cache.ndim
    return cache.transpose((0, 1) + tuple(range(3, nd)) + (2,)).reshape(depth, pool, -1, page)


def _sample_attention(pr, lw, layer, caches, page_table, n_sel):
    ck, cv, cache_mla, cache_dsa_k, cache_dsa_v, cache_dsa_idx = caches
    db, n_pages = page_table.shape
    p_len = n_pages * PAGE_SIZE
    chunk = next(c for c in (16, 8, 4, 2, 1) if n_pages % c == 0)
    n_chunks = n_pages // chunk
    any_spec = pl.BlockSpec(memory_space=pl.ANY)
    per_b3 = lambda shape: pl.BlockSpec((1,) + shape, lambda b, pt: (b, 0, 0))
    col = lambda a: a.reshape(a.shape + (1,))

    o_sb = pl.pallas_call(
        functools.partial(_sb_sample_kernel, layer=layer, n_pages=n_pages),
        grid_spec=pltpu.PrefetchScalarGridSpec(
            num_scalar_prefetch=1, grid=(db,),
            in_specs=[per_b3((SB_HEADS, LANE)), per_b3((1, LANE)), per_b3((1, LANE)), any_spec, any_spec,
                      pl.BlockSpec((LANE, 2 * LANE), lambda b, pt: (0, 0))],
            out_specs=per_b3((SB_HEADS, LANE)),
            scratch_shapes=[pltpu.VMEM((4, LANE, PAGE_SIZE), f32), pltpu.VMEM((4, LANE, PAGE_SIZE), f32),
                            pltpu.SemaphoreType.DMA((4,)), pltpu.SemaphoreType.DMA((4,)),
                            pltpu.VMEM((SB_HEADS, LANE), f32), pltpu.VMEM((SB_HEADS, LANE), f32)]),
        out_shape=jax.ShapeDtypeStruct((db, SB_HEADS, LANE), f32),
        compiler_params=_params(1),
    )(page_table, pr["q_sb"].reshape(db, SB_HEADS, LANE), pr["k_sb"].reshape(db, 1, LANE),
      pr["v_sb"].reshape(db, 1, LANE), ck, cv, _suffix_matrix(LANE))
    half = jnp.arange(SB_HEADS) // (SB_HEADS // SB_KV_HEADS)
    y_sb = jnp.where(half[None, :, None] == 0, o_sb[:, :, :HEAD_DIM], o_sb[:, :, HEAD_DIM:])
    y_sb = y_sb.reshape(db, SB_HEADS * HEAD_DIM).astype(bf16)

    kp = p_len + LANE
    ks = -(-(p_len + 1) // SEL_BLOCK) * SEL_BLOCK
    o_mla = pl.pallas_call(
        functools.partial(_mla_sample_kernel, layer=layer, n_pages=n_pages, chunk=chunk),
        grid_spec=pltpu.PrefetchScalarGridSpec(
            num_scalar_prefetch=1, grid=(db,),
            in_specs=[per_b3((MLA_HEADS, MLA_ROW)), per_b3((MLA_ROW, 1)), any_spec],
            out_specs=per_b3((MLA_ROW, MLA_HEADS)),
            scratch_shapes=[pltpu.VMEM((n_chunks, chunk, MLA_ROW, PAGE_SIZE), f32),
                            pltpu.VMEM((MLA_ROW, kp), bf16), pltpu.VMEM((MLA_HEADS, kp), f32),
                            pltpu.SemaphoreType.DMA((n_chunks,))]),
        out_shape=jax.ShapeDtypeStruct((db, MLA_ROW, MLA_HEADS), f32),
        compiler_params=_params(1),
    )(page_table, pr["q_mla"].reshape(db, MLA_HEADS, MLA_ROW), col(pr["row"]), cache_mla)
    o_mla = o_mla.transpose(0, 2, 1).reshape(db, MLA_HEADS * MLA_ROW).astype(bf16)
    y_mla = pl.pallas_call(
        _matmul_kernel,
        in_specs=[_full((db, MLA_HEADS * MLA_ROW)), _full(lw["wuv"].shape)],
        out_specs=_full((db, 512)),
        out_shape=jax.ShapeDtypeStruct((db, 512), bf16),
        grid=(1,),
        compiler_params=_params(1),
    )(o_mla, lw["wuv"])

    scores = pl.pallas_call(
        functools.partial(_idx_sample_kernel, layer=layer, n_pages=n_pages, chunk=chunk),
        grid_spec=pltpu.PrefetchScalarGridSpec(
            num_scalar_prefetch=1, grid=(db,),
            in_specs=[per_b3((IDX_HEADS, LANE)), per_b3((IDX_HEADS, 1)), per_b3((IDX_DIM, 1)), any_spec],
            out_specs=per_b3((1, ks)),
            scratch_shapes=[pltpu.VMEM((n_chunks, chunk, IDX_DIM, PAGE_SIZE), f32),
                            pltpu.SemaphoreType.DMA((n_chunks,))]),
        out_shape=jax.ShapeDtypeStruct((db, 1, ks), f32),
        compiler_params=_params(1),
    )(page_table, pr["q_idx"].reshape(db, IDX_HEADS, LANE), col(pr["ikw"][:, 64:64 + IDX_HEADS]),
      col(pr["ikw"][:, :IDX_DIM]), cache_dsa_idx)
    rb = 32 if db % 32 == 0 else db
    sel = pl.pallas_call(
        functools.partial(_select_sample_kernel, n_sel=n_sel, n_blk=ks // SEL_BLOCK, n_keys=p_len + 1),
        grid=(db // rb,),
        in_specs=[pl.BlockSpec((rb, ks), lambda i: (i, 0)), _full((LANE, 2 * LANE)), _full((LANE, LANE))],
        out_specs=pl.BlockSpec((rb, ks), lambda i: (i, 0)),
        out_shape=jax.ShapeDtypeStruct((db, ks), f32),
        scratch_shapes=[pltpu.VMEM((rb, ks), i32)],
        compiler_params=_params(1),
    )(scores.reshape(db, ks), _prefix_matrix(LANE), jnp.ones((LANE, LANE), bf16))
    o_dsa = pl.pallas_call(
        functools.partial(_dsa_sample_kernel, layer=layer, n_pages=n_pages, chunk=chunk),
        grid_spec=pltpu.PrefetchScalarGridSpec(
            num_scalar_prefetch=1, grid=(db,),
            in_specs=[per_b3((DSA_HEADS, LANE)), per_b3((1, ks)), per_b3((HEAD_DIM, 1)), per_b3((HEAD_DIM, 1)),
                      any_spec, any_spec],
            out_specs=per_b3((HEAD_DIM, DSA_HEADS)),
            scratch_shapes=[pltpu.VMEM((n_chunks, chunk, HEAD_DIM, PAGE_SIZE), f32),
                            pltpu.VMEM((n_chunks, chunk, HEAD_DIM, PAGE_SIZE), f32),
                            pltpu.VMEM((DSA_HEADS, kp), f32), pltpu.VMEM((DSA_HEADS, kp), bf16),
                            pltpu.SemaphoreType.DMA((n_chunks,)), pltpu.SemaphoreType.DMA((n_chunks,))]),
        out_shape=jax.ShapeDtypeStruct((db, HEAD_DIM, DSA_HEADS), f32),
        compiler_params=_params(1),
    )(page_table, pr["q_dsa"].reshape(db, DSA_HEADS, LANE), sel.reshape(db, 1, ks),
      col(pr["kv"][:, :HEAD_DIM]), col(pr["kv"][:, HEAD_DIM:]), cache_dsa_k, cache_dsa_v)
    y_dsa = o_dsa.transpose(0, 2, 1).reshape(db, DSA_HEADS * HEAD_DIM).astype(bf16)
    return y_sb, y_mla, y_dsa


def kernel(x_prompt, x_sample, cache_sb_k, cache_sb_v, cache_mla, cache_dsa_k, cache_dsa_v, cache_dsa_idx, page_table, meta, norm_mix, w_in, mla_q_norm, mla_kv_norm, mla_w_uq, mla_w_uk, mla_w_uv, w_branch_sb, w_branch_mla, w_branch_dsa, w_out, norm_mlp, w_up, w_down, norm_final):
    batch, seq, _ = x_prompt.shape
    db, ds, _ = x_sample.shape
    assert ds == 1, "one new token per decode entry"
    depth = w_in.shape[0]
    n_pages = page_table.shape[1]
    p_len = n_pages * PAGE_SIZE
    t_len = seq + N_META
    lp = -(-t_len // ATT_TQ) * ATT_TQ
    n_sel_p = min(TOPK_MAX, seq // 4)
    n_sel_s = min(TOPK_MAX, (p_len + ds) // 4)

    xp = jnp.concatenate([jnp.broadcast_to(meta.astype(f32)[None], (batch, N_META, D_MODEL)), x_prompt], axis=1)
    xp = jnp.pad(xp, ((0, 0), (0, lp - t_len), (0, 0))).reshape(batch * lp, D_MODEL)
    xs = x_sample.reshape(db * ds, D_MODEL)

    pos_p = jnp.arange(lp)
    pos_s = jnp.full((db,), p_len, jnp.int32)
    tabs_p = _rope_tables(pos_p, HEAD_DIM) + _rope_tables(pos_p, MLA_ROPE) + (lp // ROW_TILE,)
    tabs_s = _rope_tables(pos_s, HEAD_DIM) + _rope_tables(pos_s, MLA_ROPE) + (1,)
    place = _rope_place_matrix()
    caches = tuple(_pages_keys_on_lanes(c) for c in
                   (cache_sb_k, cache_sb_v, cache_mla, cache_dsa_k, cache_dsa_v, cache_dsa_idx))
    g_final = norm_final[None, :]

    p_rows = [[] for _ in range(6)]
    s_rows = [[] for _ in range(6)]
    for l in range(depth):
        lw = _layer_weights(l, w_in, mla_q_norm, mla_kv_norm, mla_w_uq, mla_w_uk, mla_w_uv,
                            w_branch_sb, w_branch_mla, w_branch_dsa, w_out, w_up, w_down)
        g_mix, g_mlp = norm_mix[l][None, :], norm_mlp[l][None, :]
        last = l == depth - 1
        pr = _project(xp, g_mix, lw, tabs_p, ROW_TILE, place)
        y_sb, y_mla, y_dsa = _prompt_attention(pr, lw, batch, lp, n_sel_p)
        xp = _merge_mlp(xp, y_sb, y_mla, y_dsa, g_mix, g_mlp, g_final, lw, ROW_TILE, last)
        sr = _project(xs, g_mix, lw, tabs_s, db, place)
        y_sb, y_mla, y_dsa = _sample_attention(sr, lw, l, caches, page_table, n_sel_s)
        xs = _merge_mlp(xs, y_sb, y_mla, y_dsa, g_mix, g_mlp, g_final, lw, db, last)
        for rows, r, lead in ((p_rows, pr, (batch, lp)), (s_rows, sr, (db, ds))):
            width = MLA_KV_LORA + MLA_ROPE
            rows[0].append(r["k_sb"].reshape(lead + (SB_KV_HEADS, HEAD_DIM)))
            rows[1].append(r["v_sb"].reshape(lead + (SB_KV_HEADS, HEAD_DIM)))
            rows[2].append(r["row"].reshape(lead + (MLA_ROW,))[..., :width])
            rows[3].append(r["kv"].reshape(lead + (LANE,))[..., :HEAD_DIM])
            rows[4].append(r["kv"].reshape(lead + (LANE,))[..., HEAD_DIM:])
            rows[5].append(r["ikw"].reshape(lead + (LANE,))[..., :IDX_DIM])

    y_prompt = xp.reshape(batch, lp, D_MODEL)[:, N_META:t_len]
    y_sample = xs.reshape(db, ds, D_MODEL)
    outs_p = [jnp.stack([a[:, :t_len] for a in r], axis=0) for r in p_rows]
    outs_s = [jnp.stack(r, axis=0) for r in s_rows]
    return (y_prompt, y_sample, *outs_p, *outs_s)
```

```python
import functools

import jax
import jax.numpy as jnp
from jax import lax
from jax.experimental import pallas as pl
from jax.experimental.pallas import tpu as pltpu

D_MODEL = 1024
N_META = 16
HEAD_DIM = 64
SB_HEADS = 8
SB_KV_HEADS = 2
MLA_HEADS = 8
MLA_Q_LORA = 256
MLA_KV_LORA = 192
MLA_NOPE = 64
MLA_ROPE = 32
MLA_V = 64
DSA_HEADS = 8
IDX_HEADS = 16
IDX_DIM = 64
TOPK_MAX = 256
D_FF = 4 * D_MODEL
ROPE_THETA = 10000.0
EPS = 1e-6
PAGE_SIZE = 128
SB_SCALE = HEAD_DIM ** -0.5
MLA_SCALE = (MLA_NOPE + MLA_ROPE) ** -0.5
DSA_SCALE = HEAD_DIM ** -0.5
LOG2E = 1.4426950408889634
MLA_Q_SCALE = MLA_SCALE * LOG2E
DSA_Q_SCALE = DSA_SCALE * LOG2E
IDX_SCALE = IDX_DIM ** -0.5
IDX_W_SCALE = IDX_HEADS ** -0.5

LANE = 128
VMEM_LIMIT = 56 * 1024 * 1024

MLA_ROW = 256
ROW_TILE = 256
ATT_TQ = 256
SB_TQ = 128
SB_DEAD = -104.0
NEG_BIG = -1e30
F32_MIN = -3.4028234663852886e38
INT_MIN = -(2 ** 31)

f32 = jnp.float32
bf16 = jnp.bfloat16
i32 = jnp.int32

_NT = (((1,), (1,)), ((), ()))


def _dot(a, b):
    return jnp.dot(a, b, preferred_element_type=f32)


def _dot_nt(a, b):
    return lax.dot_general(a, b, _NT, preferred_element_type=f32)


def _params(n_grid):
    return pltpu.CompilerParams(dimension_semantics=("arbitrary",) * n_grid,
                                vmem_limit_bytes=VMEM_LIMIT)


def _full(shape):
    nd = len(shape)
    return pl.BlockSpec(shape, lambda *_: (0,) * nd)


def _norm_rows(x, g):
    ms = jnp.mean(x * x, axis=-1, keepdims=True)
    return x * lax.rsqrt(ms + EPS) * g


def _lane_iota(shape):
    return lax.broadcasted_iota(i32, shape, len(shape) - 1)


def _rope_tile(x, cos, sin, dim):
    half = dim // 2
    first = (_lane_iota(x.shape) % dim) < half
    swapped = jnp.where(first, pltpu.roll(x, LANE - half, 1), pltpu.roll(x, half, 1))
    return x * cos + swapped * sin


def _log1p_exp_neg_abs(z):
    return jnp.log1p(jnp.exp(-jnp.abs(z)))


def _score_key(score):
    b = pltpu.bitcast(score, i32)
    return b ^ ((b >> 31) & 0x7FFFFFFF)


def _split_heads_to_lanes_lo(y, lo):
    zero = jnp.zeros_like(y)
    return jnp.where(lo, y, zero), jnp.where(lo, pltpu.roll(y, 64, 1), zero)


def _proj_sb_kernel(x_ref, g_ref, w_ref, q_ref, k_ref, v_ref):
    xn = _norm_rows(x_ref[...], g_ref[...]).astype(bf16)
    acc = _dot(xn, w_ref[...])
    tm = acc.shape[0]
    lo = _lane_iota((tm, LANE)) < 64
    for p in range(SB_HEADS // 2):
        y = acc[:, p * LANE:(p + 1) * LANE] * SB_SCALE
        yr = pltpu.roll(y, 64, 1)
        zero = jnp.zeros_like(y)
        if p < 2:
            even, odd = jnp.where(lo, y, zero), jnp.where(lo, yr, zero)
        else:
            even, odd = jnp.where(lo, zero, yr), jnp.where(lo, zero, y)
        q_ref[:, (2 * p) * LANE:(2 * p + 1) * LANE] = even.astype(bf16)
        q_ref[:, (2 * p + 1) * LANE:(2 * p + 2) * LANE] = odd.astype(bf16)
    k_ref[...] = acc[:, 512:640]
    v_ref[...] = acc[:, 640:768]


def _proj_mla_kernel(x_ref, g_ref, w_ref, qn_ref, kvn_ref, wuq_ref, wabs_ref, place_ref,
                     cos_ref, sin_ref, q_ref, row_ref, rowb_ref):
    xn = _norm_rows(x_ref[...], g_ref[...]).astype(bf16)
    acc = _dot(xn, w_ref[...])
    tm = acc.shape[0]
    cos, sin = cos_ref[...], sin_ref[...]
    cqn = _norm_rows(acc[:, :MLA_Q_LORA], qn_ref[...]).astype(bf16)
    qq = _dot(cqn, wuq_ref[...])
    q_lat = _dot(qq[:, :512].astype(bf16), wabs_ref[...])
    q_rot = jnp.concatenate(
        [_rope_tile(qq[:, 512 + t * LANE:512 + (t + 1) * LANE], cos, sin, MLA_ROPE) for t in range(2)],
        axis=1)
    placed = _dot((q_rot * MLA_Q_SCALE).astype(bf16), place_ref[...])
    q_ref[...] = (q_lat * MLA_Q_SCALE + placed).astype(bf16)
    g2 = acc[:, 256:512]
    lane2 = _lane_iota((tm, MLA_ROW))
    ms = jnp.sum(jnp.where(lane2 < MLA_KV_LORA, g2 * g2, 0.0), axis=-1, keepdims=True) / MLA_KV_LORA
    normed = g2 * lax.rsqrt(ms + EPS) * kvn_ref[...]
    roped = _rope_tile(g2[:, LANE:], cos, sin, MLA_ROPE)
    lane = _lane_iota((tm, LANE))
    tile2 = jnp.where(lane < 64, normed[:, LANE:], jnp.where(lane < 96, roped, 0.0))
    row = jnp.concatenate([normed[:, :LANE], tile2], axis=1)
    row_ref[...] = row
    rowb_ref[...] = row.astype(bf16)


def _proj_dsa_kernel(x_ref, g_ref, w_ref, cos_ref, sin_ref,
                     qd_ref, kv_ref, kvb_ref, qi_ref, ikw_ref, ikwb_ref):
    xn = _norm_rows(x_ref[...], g_ref[...]).astype(bf16)
    acc = _dot(xn, w_ref[...])
    tm = acc.shape[0]
    cos, sin = cos_ref[...], sin_ref[...]
    lane = _lane_iota((tm, LANE))
    lo = lane < 64
    for p in range(DSA_HEADS // 2):
        y = _rope_tile(acc[:, p * LANE:(p + 1) * LANE], cos, sin, HEAD_DIM) * DSA_Q_SCALE
        even, odd = _split_heads_to_lanes_lo(y, lo)
        qd_ref[:, (2 * p) * LANE:(2 * p + 1) * LANE] = even.astype(bf16)
        qd_ref[:, (2 * p + 1) * LANE:(2 * p + 2) * LANE] = odd.astype(bf16)
    kv = acc[:, 512:640]
    kv = jnp.where(lo, _rope_tile(kv, cos, sin, HEAD_DIM), kv)
    kv_ref[...] = kv
    kvb_ref[...] = kv.astype(bf16)
    for p in range(IDX_HEADS // 2):
        y = _rope_tile(acc[:, 640 + p * LANE:640 + (p + 1) * LANE], cos, sin, IDX_DIM) * IDX_SCALE
        even, odd = _split_heads_to_lanes_lo(y, lo)
        qi_ref[:, (2 * p) * LANE:(2 * p + 1) * LANE] = even.astype(bf16)
        qi_ref[:, (2 * p + 1) * LANE:(2 * p + 2) * LANE] = odd.astype(bf16)
    ikw = acc[:, 1664:1792]
    ikw = jnp.where(lo, _rope_tile(ikw, cos, sin, IDX_DIM),
                    jnp.where(lane < 64 + IDX_HEADS, ikw * IDX_W_SCALE, 0.0))
    ikw_ref[...] = ikw
    ikwb_ref[...] = ikw.astype(bf16)


def _merge_kernel(x_ref, g_ref, wg_ref, ysb_ref, ymla_ref, ydsa_ref,
                  wsb_ref, wmla_ref, wdsa_ref, wout_ref, o_ref):
    x = x_ref[...]
    xn = _norm_rows(x, g_ref[...]).astype(bf16)
    gates = jax.nn.sigmoid(_dot(xn, wg_ref[...]))
    m = (gates[:, :D_MODEL] * _dot(ysb_ref[...], wsb_ref[...])
         + gates[:, D_MODEL:2 * D_MODEL] * _dot(ymla_ref[...], wmla_ref[...])
         + gates[:, 2 * D_MODEL:] * _dot(ydsa_ref[...], wdsa_ref[...]))
    o_ref[...] = x + _dot(m.astype(bf16), wout_ref[...])


def _mlp_kernel(x_ref, g_ref, wup_ref, wdown_ref, gf_ref, o_ref, *, final_norm):
    x = x_ref[...]
    xn = _norm_rows(x, g_ref[...]).astype(bf16)
    h = jnp.maximum(_dot(xn, wup_ref[...]), 0.0)
    y = x + _dot((h * h).astype(bf16), wdown_ref[...])
    if final_norm:
        y = _norm_rows(y, gf_ref[...])
    o_ref[...] = y


def _matmul_kernel(a_ref, b_ref, o_ref):
    o_ref[...] = _dot(a_ref[...], b_ref[...]).astype(o_ref.dtype)


def _sb_prompt_kernel(q_ref, k_ref, v_ref, tri_ref, o_ref, carry_sc, acc_sc, *, tq):
    qi = pl.program_id(1)
    heads = SB_HEADS
    rows = heads * tq
    q = jnp.concatenate([q_ref[:, h * LANE:(h + 1) * LANE] for h in range(heads)], axis=0)
    qpos = qi * tq + lax.broadcasted_iota(i32, (rows, LANE), 0) % tq
    lane = _lane_iota((rows, LANE))
    carry_sc[...] = jnp.zeros_like(carry_sc)
    acc_sc[...] = jnp.zeros_like(acc_sc)

    def cond(s):
        j, alive = s
        return jnp.logical_and(j >= 0, alive > 0)

    def body(s):
        j, _ = s
        off = pl.multiple_of(j * LANE, LANE)
        k = k_ref[pl.ds(off, LANE), :].astype(bf16)
        v = v_ref[pl.ds(off, LANE), :].astype(bf16)
        z = _dot_nt(q, k)
        mask = (j * LANE + lane) < qpos
        soft = _log1p_exp_neg_abs(z)
        log_beta = jnp.minimum(z, 0.0) - soft
        log_fail = jnp.where(mask, jnp.minimum(-z, 0.0) - soft, 0.0)
        hi = log_fail.astype(bf16)
        lo = (log_fail - hi.astype(f32)).astype(bf16)
        tri = tri_ref[...]
        ext = _dot(hi, tri) + _dot(lo, tri)
        carry = carry_sc[...]
        a = jnp.where(mask, jnp.exp(log_beta + ext[:, :LANE] + carry), 0.0)
        acc_sc[...] += _dot(a.astype(bf16), v)
        carry = carry + ext[:, LANE:]
        carry_sc[...] = carry
        return j - 1, (jnp.max(carry) > SB_DEAD).astype(i32)

    lax.while_loop(cond, body, (qi, jnp.int32(1)))

    acc = acc_sc[...]
    lo = _lane_iota((tq, LANE)) < 64
    for p in range(heads // 2):
        e = acc[(2 * p) * tq:(2 * p + 1) * tq]
        o = acc[(2 * p + 1) * tq:(2 * p + 2) * tq]
        if p < 2:
            y = jnp.where(lo, e, pltpu.roll(o, 64, 1))
        else:
            y = jnp.where(lo, pltpu.roll(e, 64, 1), o)
        o_ref[:, p * LANE:(p + 1) * LANE] = y.astype(o_ref.dtype)


def _tile_lanes(x, n):
    return x if n == 1 else jnp.concatenate([x] * n, axis=1)


def _fold_lanes(x):
    out = x[:, :LANE]
    for t in range(1, x.shape[1] // LANE):
        out = out + x[:, t * LANE:(t + 1) * LANE]
    return out


def _softmax_block(z, k_or_v, m_prev, l_prev, acc_prev):
    n_t = z.shape[1] // LANE
    m_new = jnp.maximum(m_prev, jnp.max(z, axis=1, keepdims=True))
    alpha = jnp.exp2(m_prev - m_new)
    p = jnp.exp2(z - _tile_lanes(m_new, n_t))
    l_new = alpha * l_prev + _fold_lanes(p)
    acc_new = _tile_lanes(alpha, acc_prev.shape[-1] // LANE) * acc_prev + _dot(p.astype(bf16), k_or_v)
    return m_new, l_new, acc_new


def _softmax_heads(zs, k_or_v, m_sc, l_sc, acc_sc):
    new = [_softmax_block(z, k_or_v, m_sc[h], l_sc[h], acc_sc[h]) for h, z in enumerate(zs)]
    m_sc[...] = jnp.stack([n[0] for n in new])
    l_sc[...] = jnp.stack([n[1] for n in new])
    acc_sc[...] = jnp.stack([n[2] for n in new])


def _mla_prompt_kernel(q_ref, rows_ref, wuv_ref, o_ref, m_sc, l_sc, acc_sc, *, tq):
    qi = pl.program_id(1)
    heads = MLA_HEADS
    tk = tq
    m_sc[...] = jnp.full_like(m_sc, NEG_BIG)
    l_sc[...] = jnp.zeros_like(l_sc)
    acc_sc[...] = jnp.zeros_like(acc_sc)

    def step(j, diagonal):
        off = pl.multiple_of(j * tk, tk)
        k = rows_ref[pl.ds(off, tk), :]
        if diagonal:
            keep = _lane_iota((tq, tk)) <= lax.broadcasted_iota(i32, (tq, tk), 0)
        zs = []
        for h in range(heads):
            z = _dot_nt(q_ref[:, h * MLA_ROW:(h + 1) * MLA_ROW], k)
            zs.append(jnp.where(keep, z, NEG_BIG) if diagonal else z)
        _softmax_heads(zs, k, m_sc, l_sc, acc_sc)

    def body(j, _):
        step(j, False)
        return 0

    lax.fori_loop(0, qi, body, 0)
    step(qi, True)
    outs = []
    for h in range(heads):
        l = jnp.sum(l_sc[h], axis=1, keepdims=True)
        outs.append((acc_sc[h] / l).astype(bf16))
    o_ref[...] = _dot(jnp.concatenate(outs, axis=1), wuv_ref[...]).astype(o_ref.dtype)


def _count_ge(key_ref, cand, n_blk, tk, ones, width=1):
    span = tk * width

    def blk(jb, cnt):
        off = pl.multiple_of(jb * span, span)
        kk = key_ref[:, pl.ds(off, span)]
        for t in range(span // LANE):
            cnt = cnt + jnp.where(kk[:, t * LANE:(t + 1) * LANE] >= cand, 1.0, 0.0)
        return cnt
    cnt = lax.fori_loop(0, (n_blk + width - 1) // width, blk, jnp.zeros(cand.shape, f32))
    return _dot(cnt.astype(bf16), ones)


SEARCH_ROWS = 128
SEL_BLOCK = 1024


def _top_n_threshold(key_ref, n_blk, tk, ones, n_sel, rows, width=1):
    grp = SEARCH_ROWS if rows % SEARCH_ROWS == 0 else rows
    thrs, needs = [], []
    for r0 in range(0, rows, grp):
        ref = key_ref.at[pl.ds(r0, grp), :]

        def bit(i, t):
            cand = t + (jnp.int32(1) << (31 - i))
            return jnp.where(_count_ge(ref, cand, n_blk, tk, ones, width) >= n_sel, cand, t)

        thr = lax.fori_loop(0, 32, bit, jnp.full((grp, LANE), INT_MIN, i32))
        thrs.append(thr)
        needs.append(n_sel - _count_ge(ref, thr + 1, n_blk, tk, ones, width))
    return jnp.concatenate(thrs, axis=0), jnp.concatenate(needs, axis=0)


def _dsa_prompt_kernel(qd_ref, qi_ref, w_ref, kv_ref, ik_ref, tri_ref, ones_ref, o_ref,
                       key_sc, wb_sc, tie_sc, m_sc, l_sc, acc_sc, *, tq, n_sel):
    qb = pl.program_id(1)
    tk = tq
    n_blk = qb + 1
    lanes_per_blk = tk // LANE
    ones = ones_ref[...]
    row_pos = qb * tq + lax.broadcasted_iota(i32, (tq, tk), 0)
    lane = _lane_iota((tq, tk))

    w = w_ref[...]
    for h in range(IDX_HEADS):
        wb_sc[h] = jnp.broadcast_to(w[:, 64 + h:65 + h], (tq, LANE))

    def score_blk(j, _):
        off = pl.multiple_of(j * tk, tk)
        ik = ik_ref[pl.ds(off, tk), :]
        score = jnp.zeros((tq, tk), f32)
        for h in range(IDX_HEADS):
            s = _dot_nt(qi_ref[:, h * LANE:(h + 1) * LANE], ik)
            score = score + jnp.maximum(s, 0.0) * _tile_lanes(wb_sc[h], lanes_per_blk)
        score = score + 0.0
        score = jnp.where((j * tk + lane) <= row_pos, score, -jnp.inf)
        key_sc[:, pl.ds(off, tk)] = _score_key(score)
        return 0

    lax.fori_loop(0, n_blk, score_blk, 0)

    key_sc[:, pl.ds(pl.multiple_of(n_blk * tk, tk), tk)] = jnp.full((tq, tk), INT_MIN, i32)
    thr, need = _top_n_threshold(key_sc, n_blk, tk, ones, n_sel, tq, width=2)
    thr_f = _tile_lanes(thr, lanes_per_blk)
    need_f = _tile_lanes(need, lanes_per_blk)

    heads = DSA_HEADS
    m_sc[...] = jnp.full_like(m_sc, F32_MIN)
    l_sc[...] = jnp.zeros_like(l_sc)
    acc_sc[...] = jnp.zeros_like(acc_sc)
    tie_sc[...] = jnp.zeros_like(tie_sc)

    def att_blk(j, _):
        off = pl.multiple_of(j * tk, tk)
        kk = key_sc[:, pl.ds(off, tk)]
        eq = kk == thr_f
        pre = _dot(jnp.where(eq, 1.0, 0.0).astype(bf16), tri_ref[...])
        tie = tie_sc[...]
        rank = pre[:, :tk] + _tile_lanes(tie, lanes_per_blk)
        take = jnp.logical_or(kk > thr_f, jnp.logical_and(eq, rank < need_f))
        valid = jnp.logical_and(take, (j * tk + lane) <= row_pos)
        tie_sc[...] = tie + pre[:, tk:]
        kvb = kv_ref[pl.ds(off, tk), :]
        zs = [jnp.where(valid, _dot_nt(qd_ref[:, h * LANE:(h + 1) * LANE], kvb), -jnp.inf)
              for h in range(heads)]
        _softmax_heads(zs, kvb, m_sc, l_sc, acc_sc)
        return 0

    lax.fori_loop(0, n_blk, att_blk, 0)
    lo = _lane_iota((tq, LANE)) < 64
    outs = []
    for h in range(heads):
        outs.append(acc_sc[h] / jnp.sum(l_sc[h], axis=1, keepdims=True))
    for p in range(heads // 2):
        y = jnp.where(lo, pltpu.roll(outs[2 * p], 64, 1), outs[2 * p + 1])
        o_ref[:, p * LANE:(p + 1) * LANE] = y.astype(o_ref.dtype)


def _page_copy(cache_ref, layer, page, dst, sem):
    return pltpu.make_async_copy(cache_ref.at[layer, page], dst, sem)


def _sb_sample_kernel(pt_ref, q_ref, knew_ref, vnew_ref, ck_ref, cv_ref, tri_ref, o_ref,
                      kbuf, vbuf, sem_k, sem_v, carry_sc, acc_sc, *, layer, n_pages):
    b = pl.program_id(0)
    n_b = pl.num_programs(0)
    heads = SB_HEADS
    q = q_ref[0]
    first_slot = b & 1

    def copies(bb, j, slot):
        page = pt_ref[bb, j]
        return (_page_copy(ck_ref, layer, page, kbuf.at[slot], sem_k.at[slot]),
                _page_copy(cv_ref, layer, page, vbuf.at[slot], sem_v.at[slot]))

    def start(bb, j, slot):
        for c in copies(bb, j, slot):
            c.start()

    def wait(slot):
        for c in copies(0, 0, slot):
            c.wait()

    @pl.when(b == 0)
    def _():
        start(0, n_pages - 1, 0)

    @pl.when(b + 1 < n_b)
    def _():
        start(b + 1, n_pages - 1, 1 - first_slot)

    q_pos = n_pages * PAGE_SIZE
    new_pos = n_pages * PAGE_SIZE
    z_new = jnp.sum(q.astype(f32) * knew_ref[0].astype(bf16).astype(f32), axis=1, keepdims=True)
    new_ok = jnp.full((heads, 1), new_pos < q_pos)
    soft_new = _log1p_exp_neg_abs(z_new)
    a_new = jnp.where(new_ok, jnp.exp(jnp.minimum(z_new, 0.0) - soft_new), 0.0)
    acc_sc[...] = a_new.astype(bf16).astype(f32) * vnew_ref[0].astype(bf16).astype(f32)
    carry_sc[...] = jnp.broadcast_to(jnp.where(new_ok, jnp.minimum(-z_new, 0.0) - soft_new, 0.0),
                                     (heads, LANE))

    def cond(s):
        j, alive = s
        return jnp.logical_and(j >= 0, alive > 0)

    def body(s):
        j, _ = s
        slot = jnp.where(j == n_pages - 1, first_slot, 2 + (j & 1))
        wait(slot)

        @pl.when(j > 0)
        def _():
            start(b, j - 1, 2 + ((j - 1) & 1))

        k = kbuf[slot].astype(bf16)
        v = vbuf[slot].astype(bf16)
        z = _dot(q, k)
        soft = _log1p_exp_neg_abs(z)
        log_beta = jnp.minimum(z, 0.0) - soft
        log_fail = jnp.minimum(-z, 0.0) - soft
        hi = log_fail.astype(bf16)
        lo = (log_fail - hi.astype(f32)).astype(bf16)
        tri = tri_ref[...]
        ext = _dot(hi, tri) + _dot(lo, tri)
        carry = carry_sc[...]
        a = jnp.exp(log_beta + ext[:, :LANE] + carry)
        acc_sc[...] += _dot_nt(a.astype(bf16), v)
        carry = carry + ext[:, LANE:]
        carry_sc[...] = carry
        return j - 1, (jnp.max(carry) > SB_DEAD).astype(i32)

    j_end, _ = lax.while_loop(cond, body, (jnp.int32(n_pages - 1), jnp.int32(1)))

    @pl.when(j_end >= 0)
    def _():
        wait(2 + (j_end & 1))

    o_ref[0] = acc_sc[...]


class _PageRing:
    def __init__(self, pt_ref, cache_ref, layer, stage, sem, chunk, feat):
        self.pt_ref, self.cache_ref, self.layer = pt_ref, cache_ref, layer
        self.stage, self.sem, self.chunk, self.feat = stage, sem, chunk, feat

    def _copies(self, entry, c):
        return [pltpu.make_async_copy(
            self.cache_ref.at[self.layer, self.pt_ref[entry, c * self.chunk + i]],
            self.stage.at[c, i, pl.ds(0, self.feat), :], self.sem.at[c]) for i in range(self.chunk)]

    def start(self, entry, c):
        for cp in self._copies(entry, c):
            cp.start()

    def start_all(self, entry, n_chunks):
        for c in range(n_chunks):
            self.start(entry, c)

    def wait(self, c):
        pltpu.make_async_copy(self.cache_ref.at[self.layer, pl.ds(0, self.chunk)],
                              self.stage.at[c, :, pl.ds(0, self.feat), :], self.sem.at[c]).wait()

    def tile(self, c):
        x = self.stage[c]
        return jnp.concatenate([x[i] for i in range(self.chunk)], axis=1).astype(bf16)

    def consume(self, entry, more, n_chunks, fn, group=4):
        for g0 in range(0, n_chunks, group):
            cs = range(g0, min(g0 + group, n_chunks))
            for c in cs:
                self.wait(c)
            for c in cs:
                fn(c, self.tile(c))

            @pl.when(more)
            def _():
                for c in cs:
                    self.start(entry + 1, c)


def _first_lane_tile(col, rows):
    return jnp.where(_lane_iota((rows, LANE)) == 0, col, 0.0).astype(bf16)


def _mla_sample_kernel(pt_ref, q_ref, rcol_ref, cache_ref, o_ref, stage, kb, z_sc, sem,
                       *, layer, n_pages, chunk):
    b = pl.program_id(0)
    more = b + 1 < pl.num_programs(0)
    n_chunks = n_pages // chunk
    span = chunk * PAGE_SIZE
    p_len = n_pages * PAGE_SIZE
    width = MLA_KV_LORA + MLA_ROPE
    ring = _PageRing(pt_ref, cache_ref, layer, stage, sem, chunk, width)

    @pl.when(b == 0)
    def _():
        stage[:, :, width:, :] = jnp.zeros((n_chunks, chunk, MLA_ROW - width, LANE), f32)
        ring.start_all(0, n_chunks)

    q = q_ref[0]

    def scores(c, kc):
        kb[:, c * span:(c + 1) * span] = kc
        z_sc[:, c * span:(c + 1) * span] = _dot(q, kc)

    ring.consume(b, more, n_chunks, scores)
    knew = _first_lane_tile(rcol_ref[0], MLA_ROW)
    kb[:, pl.ds(p_len, LANE)] = knew
    z_sc[:, pl.ds(p_len, LANE)] = jnp.where(_lane_iota((MLA_HEADS, LANE)) == 0, _dot(q, knew), NEG_BIG)
    z = z_sc[...]
    p = jnp.exp2(z - jnp.max(z, axis=1, keepdims=True))
    p = (p / jnp.sum(p, axis=1, keepdims=True)).astype(bf16)
    o_ref[0] = _dot_nt(kb[...], p)


def _idx_sample_kernel(pt_ref, qi_ref, w_ref, ikcol_ref, cache_ref, o_ref, stage, sem,
                       *, layer, n_pages, chunk):
    b = pl.program_id(0)
    more = b + 1 < pl.num_programs(0)
    n_chunks = n_pages // chunk
    span = chunk * PAGE_SIZE
    ring = _PageRing(pt_ref, cache_ref, layer, stage, sem, chunk, IDX_DIM)

    @pl.when(b == 0)
    def _():
        ring.start_all(0, n_chunks)

    qk = qi_ref[0][:, :IDX_DIM]
    wb = jnp.broadcast_to(w_ref[0], (IDX_HEADS, LANE))

    def score_of(s):
        return jnp.sum(jnp.maximum(s, 0.0) * _tile_lanes(wb, s.shape[1] // LANE), axis=0, keepdims=True) + 0.0

    def body(c, kc):
        o_ref[0, :, c * span:(c + 1) * span] = score_of(_dot(qk, kc))

    ring.consume(b, more, n_chunks, body)
    p_len = n_pages * PAGE_SIZE
    s_new = score_of(_dot(qk, _first_lane_tile(ikcol_ref[0], IDX_DIM)))
    o_ref[0, :, p_len:p_len + LANE] = jnp.where(_lane_iota((1, LANE)) == 0, s_new, -jnp.inf)
    pad = o_ref.shape[2] - (p_len + LANE)
    if pad:
        o_ref[0, :, p_len + LANE:] = jnp.full((1, pad), -jnp.inf, f32)


def _select_sample_kernel(score_ref, tri_ref, ones_ref, sel_ref, key_sc, *, n_sel, n_blk, n_keys):
    rows = score_ref.shape[0]
    ones = ones_ref[...]
    tk = SEL_BLOCK

    def to_key(j, _):
        off = pl.multiple_of(j * tk, tk)
        key_sc[:, pl.ds(off, tk)] = _score_key(score_ref[:, pl.ds(off, tk)])
        return 0

    lax.fori_loop(0, n_blk, to_key, 0)
    thr, need = _top_n_threshold(key_sc, n_blk, tk, ones, n_sel, rows)
    lane = _lane_iota((rows, LANE))

    def sel_blk(j, tie):
        off = pl.multiple_of(j * tk, tk)
        kk = key_sc[:, pl.ds(off, tk)]
        sel = []
        for t in range(tk // LANE):
            kt = kk[:, t * LANE:(t + 1) * LANE]
            eq = kt == thr
            pre = _dot(jnp.where(eq, 1.0, 0.0).astype(bf16), tri_ref[...])
            take = jnp.logical_or(kt > thr, jnp.logical_and(eq, (pre[:, :LANE] + tie) < need))
            take = jnp.logical_and(take, j * tk + t * LANE + lane < n_keys)
            sel.append(jnp.where(take, 1.0, 0.0))
            tie = tie + pre[:, LANE:]
        sel_ref[:, pl.ds(off, tk)] = jnp.concatenate(sel, axis=1)
        return tie

    lax.fori_loop(0, n_blk, sel_blk, jnp.zeros((rows, LANE), f32))


def _dsa_sample_kernel(pt_ref, q_ref, sel_ref, kcol_ref, vcol_ref, ck_ref, cv_ref, o_ref,
                       kstage, vstage, z_sc, p_sc, sem_k, sem_v, *, layer, n_pages, chunk):
    b = pl.program_id(0)
    more = b + 1 < pl.num_programs(0)
    n_chunks = n_pages // chunk
    span = chunk * PAGE_SIZE
    p_len = n_pages * PAGE_SIZE
    kring = _PageRing(pt_ref, ck_ref, layer, kstage, sem_k, chunk, HEAD_DIM)
    vring = _PageRing(pt_ref, cv_ref, layer, vstage, sem_v, chunk, HEAD_DIM)

    @pl.when(b == 0)
    def _():
        kring.start_all(0, n_chunks)
        vring.start_all(0, n_chunks)

    qk = q_ref[0][:, :HEAD_DIM]

    def masked_scores(kc, off, n):
        valid = sel_ref[0, :, off:off + n] > 0.5
        z_sc[:, off:off + n] = jnp.where(valid, _dot(qk, kc), NEG_BIG)

    kring.consume(b, more, n_chunks, lambda c, kc: masked_scores(kc, c * span, span))
    masked_scores(_first_lane_tile(kcol_ref[0], HEAD_DIM), p_len, LANE)
    z = z_sc[...]
    p = jnp.where(z > 0.5 * NEG_BIG, jnp.exp2(z - jnp.max(z, axis=1, keepdims=True)), 0.0)
    p_sc[...] = (p / jnp.sum(p, axis=1, keepdims=True)).astype(bf16)

    parts = [_dot_nt(_first_lane_tile(vcol_ref[0], HEAD_DIM), p_sc[:, p_len:p_len + LANE])]
    vring.consume(b, more, n_chunks,
                  lambda c, vc: parts.append(_dot_nt(vc, p_sc[:, c * span:(c + 1) * span])))
    o_ref[0] = sum(parts[1:], parts[0])


def _rope_tables(pos, dim):
    half = dim // 2
    inv_freq = ROPE_THETA ** (-jnp.arange(half, dtype=f32) / half)
    ang = pos.astype(f32)[:, None] * inv_freq[None, :]
    cos, sin = jnp.cos(ang), jnp.sin(ang)
    reps = LANE // dim
    return (jnp.tile(jnp.concatenate([cos, cos], axis=-1), (1, reps)),
            jnp.tile(jnp.concatenate([-sin, sin], axis=-1), (1, reps)))


def _layer_weights(l, w_in, mla_q_norm, mla_kv_norm, mla_w_uq, mla_w_uk, mla_w_uv,
                   w_branch_sb, w_branch_mla, w_branch_dsa, w_out, w_up, w_down):
    w = w_in[l].astype(bf16)
    c = 0
    cols = {}
    for name, n in (("sb", 768), ("mla_cq", 256), ("mla_kv", 224), ("dsa_q", 512), ("dsa_kv", 128),
                    ("idx_q", 1024), ("idx_kw", 80), ("gates", 3 * D_MODEL)):
        cols[name] = w[:, c:c + n]
        c += n
    pad = lambda a, n: jnp.pad(a, ((0, 0), (0, n - a.shape[1])))
    w_mla = jnp.concatenate([cols["mla_cq"], pad(cols["mla_kv"], MLA_ROW)], axis=1)
    w_dsa = jnp.concatenate([cols["dsa_q"], cols["dsa_kv"], cols["idx_q"], pad(cols["idx_kw"], LANE)], axis=1)
    uq = mla_w_uq[l].astype(bf16)
    wuq = jnp.concatenate([uq[:, :, :MLA_NOPE].reshape(MLA_Q_LORA, MLA_HEADS * MLA_NOPE),
                           uq[:, :, MLA_NOPE:].reshape(MLA_Q_LORA, MLA_HEADS * MLA_ROPE)], axis=1)
    uk = mla_w_uk[l].astype(bf16)
    uv = mla_w_uv[l].astype(bf16)
    wabs = jnp.zeros((MLA_HEADS * MLA_NOPE, MLA_HEADS * MLA_ROW), bf16)
    wuv = jnp.zeros((MLA_HEADS * MLA_ROW, MLA_HEADS * MLA_V), bf16)
    for h in range(MLA_HEADS):
        wabs = wabs.at[h * MLA_NOPE:(h + 1) * MLA_NOPE, h * MLA_ROW:h * MLA_ROW + MLA_KV_LORA].set(uk[:, h, :].T)
        wuv = wuv.at[h * MLA_ROW:h * MLA_ROW + MLA_KV_LORA, h * MLA_V:(h + 1) * MLA_V].set(uv[:, h, :])
    return dict(
        w_sb=cols["sb"], w_mla=w_mla, w_dsa=w_dsa, w_gates=cols["gates"],
        qn=mla_q_norm[l][None, :], kvn=jnp.pad(mla_kv_norm[l], (0, MLA_ROW - MLA_KV_LORA))[None, :],
        wuq=wuq, wabs=wabs, wuv=wuv,
        w_bsb=w_branch_sb[l].astype(bf16), w_bmla=w_branch_mla[l].astype(bf16),
        w_bdsa=w_branch_dsa[l].astype(bf16), w_out=w_out[l].astype(bf16),
        w_up=w_up[l].astype(bf16), w_down=w_down[l].astype(bf16))


def _rope_place_matrix():
    src = jnp.arange(MLA_HEADS * MLA_ROPE)
    dst = (src // MLA_ROPE) * MLA_ROW + MLA_KV_LORA + src % MLA_ROPE
    return jnp.zeros((MLA_HEADS * MLA_ROPE, MLA_HEADS * MLA_ROW), bf16).at[src, dst].set(1)


def _suffix_matrix(n):
    j = jnp.arange(n)[:, None]
    s = jnp.arange(n)[None, :]
    return jnp.concatenate([(j > s), jnp.ones((n, LANE), bool)], axis=1).astype(bf16)


def _prefix_matrix(n):
    j = jnp.arange(n)[:, None]
    s = jnp.arange(n)[None, :]
    return jnp.concatenate([(j < s), jnp.ones((n, LANE), bool)], axis=1).astype(bf16)


def _row_call(kernel, n_rows, tm, row_ins, const_ins, pos_ins, outs):
    in_specs = [pl.BlockSpec((tm, a.shape[1]), lambda i: (i, 0)) for a in row_ins]
    in_specs += [_full(a.shape) for a in const_ins]
    for a, nb in pos_ins:
        in_specs.append(pl.BlockSpec((tm, a.shape[1]), functools.partial(lambda i, nb: (i % nb, 0), nb=nb)))
    return pl.pallas_call(
        kernel,
        grid=(n_rows // tm,),
        in_specs=in_specs,
        out_specs=[pl.BlockSpec((tm, c), lambda i: (i, 0)) for c, _ in outs],
        out_shape=[jax.ShapeDtypeStruct((n_rows, c), d) for c, d in outs],
        compiler_params=_params(1),
    )(*row_ins, *const_ins, *[a for a, _ in pos_ins])


def _project(x, g, lw, tabs, tm, place):
    n = x.shape[0]
    cos64, sin64, cos32, sin32, nb = tabs
    q_sb, k_sb, v_sb = _row_call(
        _proj_sb_kernel, n, tm, [x], [g, lw["w_sb"]], [],
        [(SB_HEADS * LANE, bf16), (LANE, f32), (LANE, f32)])
    q_mla, row, rowb = _row_call(
        _proj_mla_kernel, n, tm, [x],
        [g, lw["w_mla"], lw["qn"], lw["kvn"], lw["wuq"], lw["wabs"], place],
        [(cos32, nb), (sin32, nb)],
        [(MLA_HEADS * MLA_ROW, bf16), (MLA_ROW, f32), (MLA_ROW, bf16)])
    q_dsa, kv, kvb, q_idx, ikw, ikwb = _row_call(
        _proj_dsa_kernel, n, tm, [x], [g, lw["w_dsa"]], [(cos64, nb), (sin64, nb)],
        [(DSA_HEADS * LANE, bf16), (LANE, f32), (LANE, bf16),
         (IDX_HEADS * LANE, bf16), (LANE, f32), (LANE, bf16)])
    return dict(q_sb=q_sb, k_sb=k_sb, v_sb=v_sb, q_mla=q_mla, row=row, rowb=rowb,
                q_dsa=q_dsa, kv=kv, kvb=kvb, q_idx=q_idx, ikw=ikw, ikwb=ikwb)


def _merge_mlp(x, y_sb, y_mla, y_dsa, g_mix, g_mlp, g_final, lw, tm, final_norm):
    n = x.shape[0]
    x1 = pl.pallas_call(
        _merge_kernel,
        grid=(n // tm,),
        in_specs=[pl.BlockSpec((tm, D_MODEL), lambda i: (i, 0)), _full(g_mix.shape), _full(lw["w_gates"].shape),
                  pl.BlockSpec((tm, 512), lambda i: (i, 0)), pl.BlockSpec((tm, 512), lambda i: (i, 0)),
                  pl.BlockSpec((tm, 512), lambda i: (i, 0)),
                  _full(lw["w_bsb"].shape), _full(lw["w_bmla"].shape), _full(lw["w_bdsa"].shape),
                  _full(lw["w_out"].shape)],
        out_specs=pl.BlockSpec((tm, D_MODEL), lambda i: (i, 0)),
        out_shape=jax.ShapeDtypeStruct((n, D_MODEL), f32),
        compiler_params=_params(1),
    )(x, g_mix, lw["w_gates"], y_sb, y_mla, y_dsa, lw["w_bsb"], lw["w_bmla"], lw["w_bdsa"], lw["w_out"])
    return pl.pallas_call(
        functools.partial(_mlp_kernel, final_norm=final_norm),
        grid=(n // tm,),
        in_specs=[pl.BlockSpec((tm, D_MODEL), lambda i: (i, 0)), _full(g_mlp.shape),
                  _full(lw["w_up"].shape), _full(lw["w_down"].shape), _full(g_final.shape)],
        out_specs=pl.BlockSpec((tm, D_MODEL), lambda i: (i, 0)),
        out_shape=jax.ShapeDtypeStruct((n, D_MODEL), f32),
        compiler_params=_params(1),
    )(x1, g_mlp, lw["w_up"], lw["w_down"], g_final)


def _prompt_attention(pr, lw, batch, lp, n_sel):
    n = batch * lp
    nq = lp // SB_TQ
    y_sb = pl.pallas_call(
        functools.partial(_sb_prompt_kernel, tq=SB_TQ),
        grid=(batch, nq),
        in_specs=[pl.BlockSpec((SB_TQ, SB_HEADS * LANE), lambda b, i: (b * nq + i, 0)),
                  pl.BlockSpec((lp, LANE), lambda b, i: (b, 0)),
                  pl.BlockSpec((lp, LANE), lambda b, i: (b, 0)),
                  _full((LANE, 2 * LANE))],
        out_specs=pl.BlockSpec((SB_TQ, 512), lambda b, i: (b * nq + i, 0)),
        out_shape=jax.ShapeDtypeStruct((n, 512), bf16),
        scratch_shapes=[pltpu.VMEM((SB_HEADS * SB_TQ, LANE), f32), pltpu.VMEM((SB_HEADS * SB_TQ, LANE), f32)],
        compiler_params=_params(2),
    )(pr["q_sb"], pr["k_sb"], pr["v_sb"], _suffix_matrix(LANE))
    tq = ATT_TQ
    nq = lp // tq
    y_mla = pl.pallas_call(
        functools.partial(_mla_prompt_kernel, tq=tq),
        grid=(batch, nq),
        in_specs=[pl.BlockSpec((tq, MLA_HEADS * MLA_ROW), lambda b, i: (b * nq + i, 0)),
                  pl.BlockSpec((lp, MLA_ROW), lambda b, i: (b, 0)),
                  _full(lw["wuv"].shape)],
        out_specs=pl.BlockSpec((tq, 512), lambda b, i: (b * nq + i, 0)),
        out_shape=jax.ShapeDtypeStruct((n, 512), bf16),
        scratch_shapes=[pltpu.VMEM((MLA_HEADS, tq, LANE), f32), pltpu.VMEM((MLA_HEADS, tq, LANE), f32),
                        pltpu.VMEM((MLA_HEADS, tq, MLA_ROW), f32)],
        compiler_params=_params(2),
    )(pr["q_mla"], pr["rowb"], lw["wuv"])
    y_dsa = pl.pallas_call(
        functools.partial(_dsa_prompt_kernel, tq=tq, n_sel=n_sel),
        grid=(batch, nq),
        in_specs=[pl.BlockSpec((tq, DSA_HEADS * LANE), lambda b, i: (b * nq + i, 0)),
                  pl.BlockSpec((tq, IDX_HEADS * LANE), lambda b, i: (b * nq + i, 0)),
                  pl.BlockSpec((tq, LANE), lambda b, i: (b * nq + i, 0)),
                  pl.BlockSpec((lp, LANE), lambda b, i: (b, 0)),
                  pl.BlockSpec((lp, LANE), lambda b, i: (b, 0)),
                  _full((tq, tq + LANE)), _full((LANE, LANE))],
        out_specs=pl.BlockSpec((tq, 512), lambda b, i: (b * nq + i, 0)),
        out_shape=jax.ShapeDtypeStruct((n, 512), bf16),
        scratch_shapes=[pltpu.VMEM((tq, lp + tq), i32), pltpu.VMEM((IDX_HEADS, tq, LANE), f32),
                        pltpu.VMEM((tq, LANE), f32),
                        pltpu.VMEM((DSA_HEADS, tq, LANE), f32), pltpu.VMEM((DSA_HEADS, tq, LANE), f32),
                        pltpu.VMEM((DSA_HEADS, tq, LANE), f32)],
        compiler_params=_params(2),
    )(pr["q_dsa"], pr["q_idx"], pr["ikw"], pr["kvb"], pr["ikwb"], _prefix_matrix(tq),
      jnp.ones((LANE, LANE), bf16))
    return y_sb, y_mla, y_dsa


def _pages_keys_on_lanes(cache):
    depth, pool, page = cache.shape[:3]
    nd = cache.ndim
    return cache.transpose((0, 1) + tuple(range(3, nd)) + (2,)).reshape(depth, pool, -1, page)


def _sample_attention(pr, lw, layer, caches, page_table, n_sel):
    ck, cv, cache_mla, cache_dsa_k, cache_dsa_v, cache_dsa_idx = caches
    db, n_pages = page_table.shape
    p_len = n_pages * PAGE_SIZE
    chunk = next(c for c in (16, 8, 4, 2, 1) if n_pages % c == 0)
    n_chunks = n_pages // chunk
    any_spec = pl.BlockSpec(memory_space=pl.ANY)
    per_b3 = lambda shape: pl.BlockSpec((1,) + shape, lambda b, pt: (b, 0, 0))
    col = lambda a: a.reshape(a.shape + (1,))

    o_sb = pl.pallas_call(
        functools.partial(_sb_sample_kernel, layer=layer, n_pages=n_pages),
        grid_spec=pltpu.PrefetchScalarGridSpec(
            num_scalar_prefetch=1, grid=(db,),
            in_specs=[per_b3((SB_HEADS, LANE)), per_b3((1, LANE)), per_b3((1, LANE)), any_spec, any_spec,
                      pl.BlockSpec((LANE, 2 * LANE), lambda b, pt: (0, 0))],
            out_specs=per_b3((SB_HEADS, LANE)),
            scratch_shapes=[pltpu.VMEM((4, LANE, PAGE_SIZE), f32), pltpu.VMEM((4, LANE, PAGE_SIZE), f32),
                            pltpu.SemaphoreType.DMA((4,)), pltpu.SemaphoreType.DMA((4,)),
                            pltpu.VMEM((SB_HEADS, LANE), f32), pltpu.VMEM((SB_HEADS, LANE), f32)]),
        out_shape=jax.ShapeDtypeStruct((db, SB_HEADS, LANE), f32),
        compiler_params=_params(1),
    )(page_table, pr["q_sb"].reshape(db, SB_HEADS, LANE), pr["k_sb"].reshape(db, 1, LANE),
      pr["v_sb"].reshape(db, 1, LANE), ck, cv, _suffix_matrix(LANE))
    half = jnp.arange(SB_HEADS) // (SB_HEADS // SB_KV_HEADS)
    y_sb = jnp.where(half[None, :, None] == 0, o_sb[:, :, :HEAD_DIM], o_sb[:, :, HEAD_DIM:])
    y_sb = y_sb.reshape(db, SB_HEADS * HEAD_DIM).astype(bf16)

    kp = p_len + LANE
    ks = -(-(p_len + 1) // SEL_BLOCK) * SEL_BLOCK
    o_mla = pl.pallas_call(
        functools.partial(_mla_sample_kernel, layer=layer, n_pages=n_pages, chunk=chunk),
        grid_spec=pltpu.PrefetchScalarGridSpec(
            num_scalar_prefetch=1, grid=(db,),
            in_specs=[per_b3((MLA_HEADS, MLA_ROW)), per_b3((MLA_ROW, 1)), any_spec],
            out_specs=per_b3((MLA_ROW, MLA_HEADS)),
            scratch_shapes=[pltpu.VMEM((n_chunks, chunk, MLA_ROW, PAGE_SIZE), f32),
                            pltpu.VMEM((MLA_ROW, kp), bf16), pltpu.VMEM((MLA_HEADS, kp), f32),
                            pltpu.SemaphoreType.DMA((n_chunks,))]),
        out_shape=jax.ShapeDtypeStruct((db, MLA_ROW, MLA_HEADS), f32),
        compiler_params=_params(1),
    )(page_table, pr["q_mla"].reshape(db, MLA_HEADS, MLA_ROW), col(pr["row"]), cache_mla)
    o_mla = o_mla.transpose(0, 2, 1).reshape(db, MLA_HEADS * MLA_ROW).astype(bf16)
    y_mla = pl.pallas_call(
        _matmul_kernel,
        in_specs=[_full((db, MLA_HEADS * MLA_ROW)), _full(lw["wuv"].shape)],
        out_specs=_full((db, 512)),
        out_shape=jax.ShapeDtypeStruct((db, 512), bf16),
        grid=(1,),
        compiler_params=_params(1),
    )(o_mla, lw["wuv"])

    scores = pl.pallas_call(
        functools.partial(_idx_sample_kernel, layer=layer, n_pages=n_pages, chunk=chunk),
        grid_spec=pltpu.PrefetchScalarGridSpec(
            num_scalar_prefetch=1, grid=(db,),
            in_specs=[per_b3((IDX_HEADS, LANE)), per_b3((IDX_HEADS, 1)), per_b3((IDX_DIM, 1)), any_spec],
            out_specs=per_b3((1, ks)),
            scratch_shapes=[pltpu.VMEM((n_chunks, chunk, IDX_DIM, PAGE_SIZE), f32),
                            pltpu.SemaphoreType.DMA((n_chunks,))]),
        out_shape=jax.ShapeDtypeStruct((db, 1, ks), f32),
        compiler_params=_params(1),
    )(page_table, pr["q_idx"].reshape(db, IDX_HEADS, LANE), col(pr["ikw"][:, 64:64 + IDX_HEADS]),
      col(pr["ikw"][:, :IDX_DIM]), cache_dsa_idx)
    rb = 32 if db % 32 == 0 else db
    sel = pl.pallas_call(
        functools.partial(_select_sample_kernel, n_sel=n_sel, n_blk=ks // SEL_BLOCK, n_keys=p_len + 1),
        grid=(db // rb,),
        in_specs=[pl.BlockSpec((rb, ks), lambda i: (i, 0)), _full((LANE, 2 * LANE)), _full((LANE, LANE))],
        out_specs=pl.BlockSpec((rb, ks), lambda i: (i, 0)),
        out_shape=jax.ShapeDtypeStruct((db, ks), f32),
        scratch_shapes=[pltpu.VMEM((rb, ks), i32)],
        compiler_params=_params(1),
    )(scores.reshape(db, ks), _prefix_matrix(LANE), jnp.ones((LANE, LANE), bf16))
    o_dsa = pl.pallas_call(
        functools.partial(_dsa_sample_kernel, layer=layer, n_pages=n_pages, chunk=chunk),
        grid_spec=pltpu.PrefetchScalarGridSpec(
            num_scalar_prefetch=1, grid=(db,),
            in_specs=[per_b3((DSA_HEADS, LANE)), per_b3((1, ks)), per_b3((HEAD_DIM, 1)), per_b3((HEAD_DIM, 1)),
                      any_spec, any_spec],
            out_specs=per_b3((HEAD_DIM, DSA_HEADS)),
            scratch_shapes=[pltpu.VMEM((n_chunks, chunk, HEAD_DIM, PAGE_SIZE), f32),
                            pltpu.VMEM((n_chunks, chunk, HEAD_DIM, PAGE_SIZE), f32),
                            pltpu.VMEM((DSA_HEADS, kp), f32), pltpu.VMEM((DSA_HEADS, kp), bf16),
                            pltpu.SemaphoreType.DMA((n_chunks,)), pltpu.SemaphoreType.DMA((n_chunks,))]),
        out_shape=jax.ShapeDtypeStruct((db, HEAD_DIM, DSA_HEADS), f32),
        compiler_params=_params(1),
    )(page_table, pr["q_dsa"].reshape(db, DSA_HEADS, LANE), sel.reshape(db, 1, ks),
      col(pr["kv"][:, :HEAD_DIM]), col(pr["kv"][:, HEAD_DIM:]), cache_dsa_k, cache_dsa_v)
    y_dsa = o_dsa.transpose(0, 2, 1).reshape(db, DSA_HEADS * HEAD_DIM).astype(bf16)
    return y_sb, y_mla, y_dsa


def kernel(x_prompt, x_sample, cache_sb_k, cache_sb_v, cache_mla, cache_dsa_k, cache_dsa_v, cache_dsa_idx, page_table, meta, norm_mix, w_in, mla_q_norm, mla_kv_norm, mla_w_uq, mla_w_uk, mla_w_uv, w_branch_sb, w_branch_mla, w_branch_dsa, w_out, norm_mlp, w_up, w_down, norm_final):
    batch, seq, _ = x_prompt.shape
    db, ds, _ = x_sample.shape
    assert ds == 1, "one new token per decode entry"
    depth = w_in.shape[0]
    n_pages = page_table.shape[1]
    p_len = n_pages * PAGE_SIZE
    t_len = seq + N_META
    lp = -(-t_len // ATT_TQ) * ATT_TQ
    n_sel_p = min(TOPK_MAX, seq // 4)
    n_sel_s = min(TOPK_MAX, (p_len + ds) // 4)

    xp = jnp.concatenate([jnp.broadcast_to(meta.astype(f32)[None], (batch, N_META, D_MODEL)), x_prompt], axis=1)
    xp = jnp.pad(xp, ((0, 0), (0, lp - t_len), (0, 0))).reshape(batch * lp, D_MODEL)
    xs = x_sample.reshape(db * ds, D_MODEL)

    pos_p = jnp.arange(lp)
    pos_s = jnp.full((db,), p_len, jnp.int32)
    tabs_p = _rope_tables(pos_p, HEAD_DIM) + _rope_tables(pos_p, MLA_ROPE) + (lp // ROW_TILE,)
    tabs_s = _rope_tables(pos_s, HEAD_DIM) + _rope_tables(pos_s, MLA_ROPE) + (1,)
    place = _rope_place_matrix()
    caches = tuple(_pages_keys_on_lanes(c) for c in
                   (cache_sb_k, cache_sb_v, cache_mla, cache_dsa_k, cache_dsa_v, cache_dsa_idx))
    g_final = norm_final[None, :]

    p_rows = [[] for _ in range(6)]
    s_rows = [[] for _ in range(6)]
    for l in range(depth):
        lw = _layer_weights(l, w_in, mla_q_norm, mla_kv_norm, mla_w_uq, mla_w_uk, mla_w_uv,
                            w_branch_sb, w_branch_mla, w_branch_dsa, w_out, w_up, w_down)
        g_mix, g_mlp = norm_mix[l][None, :], norm_mlp[l][None, :]
        last = l == depth - 1
        pr = _project(xp, g_mix, lw, tabs_p, ROW_TILE, place)
        y_sb, y_mla, y_dsa = _prompt_attention(pr, lw, batch, lp, n_sel_p)
        xp = _merge_mlp(xp, y_sb, y_mla, y_dsa, g_mix, g_mlp, g_final, lw, ROW_TILE, last)
        sr = _project(xs, g_mix, lw, tabs_s, db, place)
        y_sb, y_mla, y_dsa = _sample_attention(sr, lw, l, caches, page_table, n_sel_s)
        xs = _merge_mlp(xs, y_sb, y_mla, y_dsa, g_mix, g_mlp, g_final, lw, db, last)
        for rows, r, lead in ((p_rows, pr, (batch, lp)), (s_rows, sr, (db, ds))):
            width = MLA_KV_LORA + MLA_ROPE
            rows[0].append(r["k_sb"].reshape(lead + (SB_KV_HEADS, HEAD_DIM)))
            rows[1].append(r["v_sb"].reshape(lead + (SB_KV_HEADS, HEAD_DIM)))
            rows[2].append(r["row"].reshape(lead + (MLA_ROW,))[..., :width])
            rows[3].append(r["kv"].reshape(lead + (LANE,))[..., :HEAD_DIM])
            rows[4].append(r["kv"].reshape(lead + (LANE,))[..., HEAD_DIM:])
            rows[5].append(r["ikw"].reshape(lead + (LANE,))[..., :IDX_DIM])

    y_prompt = xp.reshape(batch, lp, D_MODEL)[:, N_META:t_len]
    y_sample = xs.reshape(db, ds, D_MODEL)
    outs_p = [jnp.stack([a[:, :t_len] for a in r], axis=0) for r in p_rows]
    outs_s = [jnp.stack(r, axis=0) for r in s_rows]
    return (y_prompt, y_sample, *outs_p, *outs_s)
```
